```python
import math
import jax, jax.numpy as jnp
from jax import lax
import numpy as np

D_MODEL = 1024
BATCH = 8
SEQ = 2048
DEPTH = 1
DEC_BATCH = 16
DEC_SEQ = 16
PAST_LEN = 1024

CHUNK = 64
QB = 128
N_HEADS_A = 8
HEAD_DIM_A = 64
W_A = N_HEADS_A * 2 * HEAD_DIM_A
N_HEADS_B = 16
HEAD_SIZE_B = 64
W_B = N_HEADS_B * HEAD_SIZE_B
D_DECAY_LORA = 64
D_AAA_LORA = 64
D_GATE_LORA = 128
C_SHIFT = 3 * W_B + D_DECAY_LORA + D_AAA_LORA + D_GATE_LORA
SPLIT_B = (W_B, 2 * W_B, 3 * W_B, 3 * W_B + D_DECAY_LORA, 3 * W_B + D_DECAY_LORA + D_AAA_LORA)
GN_EPS = 64e-5
D_IN = 3 * W_A + C_SHIFT + 2 * D_MODEL
IN_SPLIT = (W_A, 2 * W_A, 3 * W_A, 3 * W_A + C_SHIFT)
N_KEYS = 128
N_EXPERTS = N_KEYS * N_KEYS
PEER_HEADS = 8
PEER_TOPK = 16
D_KEY = 256
TOKEN_BLOCK = 128
NEG = -1e30

kernel_name = "hybrid_diffattn_rwkv7_peer_stream_step"


def rms_norm(x, g, eps=1e-6):
    xf = x.astype(jnp.float32)
    y = xf * lax.rsqrt(jnp.mean(xf * xf, axis=-1, keepdims=True) + eps)
    return (y * g.astype(jnp.float32)).astype(x.dtype)


def alibi_slopes():
    return jnp.asarray([2.0 ** (-8.0 * (h + 1) / N_HEADS_A) for h in range(N_HEADS_A)], jnp.float32)


def lambda_init(layer_idx):
    return 0.8 - 0.6 * math.exp(-0.3 * layer_idx)


def diff_attn_block(q, k, v, q_pos, k_pos, lam, slopes):
    s = jnp.einsum('bqhcd,bkhcd->bhcqk', q, k).astype(jnp.float32) * (HEAD_DIM_A ** -0.5)
    dist = jnp.abs(q_pos[:, None] - k_pos[None, :]).astype(jnp.float32)
    visible = (k_pos[None, :] // CHUNK) <= (q_pos[:, None] // CHUNK)
    s = jnp.where(visible, s - slopes[None, :, None, None, None] * dist, NEG)
    p = jax.nn.softmax(s, axis=-1)
    attn = p[:, :, 0] - lam * p[:, :, 1]
    return jnp.einsum('bhqk,bkhe->bqhe', attn.astype(v.dtype), v)


def diff_attn_prompt(q, k, v, lam, slopes):
    B, S = q.shape[0], q.shape[1]
    nb = S // QB
    qb = q.reshape(B, nb, QB, N_HEADS_A, 2, HEAD_DIM_A).transpose(1, 0, 2, 3, 4, 5)
    pos = jnp.arange(S, dtype=jnp.int32)
    pb = pos.reshape(nb, QB)
    out = lax.map(lambda a: diff_attn_block(a[0], k, v, a[1], pos, lam, slopes), (qb, pb))
    return out.transpose(1, 0, 2, 3, 4).reshape(B, S, N_HEADS_A, 2 * HEAD_DIM_A)


def wkv_scan(r, decay, k, v, kk, a, S0):
    def step(S, inp):
        r_t, w_t, k_t, v_t, kk_t, a_t = inp
        sa = jnp.einsum('bhvk,bhk->bhv', S, kk_t)
        S = (S * w_t[:, :, None, :] - sa[..., None] * (kk_t * a_t)[:, :, None, :]
             + v_t[..., None] * k_t[:, :, None, :])
        return S, jnp.einsum('bhvk,bhk->bhv', S, r_t)
    xs = tuple(jnp.moveaxis(t, 1, 0) for t in (r, decay, k, v, kk, a))
    S, ys = lax.scan(step, S0, xs)
    return jnp.moveaxis(ys, 0, 1), S


def rwkv_branch(pr, prev_row, S0, prm):
    B, T, _ = pr.shape
    f32 = jnp.float32
    shifted = jnp.concatenate([prev_row[:, None, :].astype(pr.dtype), pr[:, :-1]], axis=1)
    xs = pr + (shifted - pr) * prm['shift_mu']
    r, k, v, xw, xa, xg = jnp.split(xs, SPLIT_B, axis=-1)
    w = -jax.nn.softplus(-(prm['decay_w0'] + jnp.tanh(xw) @ prm['decay_lora_up']).astype(f32)) - 0.5
    decay = jnp.exp(-jnp.exp(w))
    a = jax.nn.sigmoid((prm['iclr_a0'] + xa @ prm['iclr_lora_up']).astype(f32))
    g = jax.nn.sigmoid(xg) @ prm['gate_lora_up']
    heads = lambda t: t.reshape(B, T, N_HEADS_B, HEAD_SIZE_B).astype(f32)
    r_h, k_h, v_h, w_h, a_h = heads(r), heads(k), heads(v), heads(decay), heads(a)
    kk = heads(k * prm['k_k'])
    kk = kk / jnp.maximum(jnp.sqrt(jnp.sum(kk * kk, axis=-1, keepdims=True)), 1e-12)
    k_a = prm['k_a'].reshape(N_HEADS_B, HEAD_SIZE_B).astype(f32)
    k_mod = k_h * (1.0 + (a_h - 1.0) * k_a)
    y, S = wkv_scan(r_h, w_h, k_mod, v_h, kk, a_h, S0.astype(f32))
    mu = jnp.mean(y, axis=-1, keepdims=True)
    var = jnp.mean(jnp.square(y - mu), axis=-1, keepdims=True)
    yn = ((y - mu) * lax.rsqrt(var + GN_EPS)).reshape(B, T, W_B)
    yn = yn * prm['lnx_g'].astype(f32) + prm['lnx_b'].astype(f32)
    bonus = (jnp.sum(r_h * k_mod * prm['r_k'].astype(f32), axis=-1, keepdims=True) * v_h).reshape(B, T, W_B)
    out = ((yn + bonus) * g.astype(f32)).astype(pr.dtype)
    return out, S.astype(pr.dtype), pr[:, -1]


def peer(x, w_query, sub_keys, expert_u, expert_v):
    T = x.shape[0]
    pad = (-T) % TOKEN_BLOCK
    blocks = jnp.pad(x, ((0, pad), (0, 0))).reshape(-1, TOKEN_BLOCK, D_MODEL)

    def one(xb):
        q = (xb @ w_query).reshape(TOKEN_BLOCK, PEER_HEADS, 2, D_KEY // 2)
        s = jnp.einsum('thcd,hcnd->thcn', q, sub_keys).astype(jnp.float32)
        vals, idx = lax.top_k(s, PEER_TOPK)
        cand = vals[:, :, 0, :, None] + vals[:, :, 1, None, :]
        cid = idx[:, :, 0, :, None] * N_KEYS + idx[:, :, 1, None, :]
        top, sel = lax.top_k(cand.reshape(TOKEN_BLOCK, PEER_HEADS, PEER_TOPK * PEER_TOPK), PEER_TOPK)
        eid = jnp.take_along_axis(cid.reshape(TOKEN_BLOCK, PEER_HEADS, PEER_TOPK * PEER_TOPK), sel, axis=-1)
        gate = jax.nn.softmax(top, axis=-1)
        hid = jax.nn.gelu(jnp.einsum('thkd,td->thk', expert_u[eid], xb), approximate=False)
        return jnp.einsum('thk,thkd->td', (gate * hid).astype(xb.dtype), expert_v[eid])

    return lax.map(one, blocks).reshape(-1, D_MODEL)[:T]


def trunk_layer(x, c, prm, lam_init, slopes, past):
    B, T, _ = x.shape
    ada = jax.nn.silu(c) @ prm['w_ada'] + prm['b_ada']
    sh1, sc1, gt1, sh2, sc2, gt2 = [t[:, None, :] for t in jnp.split(ada, 6, axis=-1)]
    h = rms_norm(x, prm['norm_mix_g']) * (1.0 + sc1) + sh1
    p = h @ prm['w_in']
    q, k, v, pr, gates = jnp.split(p, IN_SPLIT, axis=-1)
    q = rms_norm(q.reshape(B, T, N_HEADS_A, 2, HEAD_DIM_A), prm['q_norm_g'])
    k = rms_norm(k.reshape(B, T, N_HEADS_A, 2, HEAD_DIM_A), prm['k_norm_g'])
    v = v.reshape(B, T, N_HEADS_A, 2 * HEAD_DIM_A)
    lq = prm['lambda_qk'].astype(jnp.float32)
    lam = jnp.exp(jnp.sum(lq[0] * lq[1])) - jnp.exp(jnp.sum(lq[2] * lq[3])) + lam_init
    if past is None:
        o_a = diff_attn_prompt(q, k, v, lam, slopes)
        prev = jnp.zeros((B, C_SHIFT), x.dtype)
        S0 = jnp.zeros((B, N_HEADS_B, HEAD_SIZE_B, HEAD_SIZE_B), jnp.float32)
    else:
        k_past, v_past, S0, prev = past
        P = k_past.shape[1]
        k_all = jnp.concatenate([k_past.reshape(B, P, N_HEADS_A, 2, HEAD_DIM_A).astype(k.dtype), k], axis=1)
        v_all = jnp.concatenate([v_past.astype(v.dtype), v], axis=1)
        q_pos = P + jnp.arange(T, dtype=jnp.int32)
        k_pos = jnp.arange(P + T, dtype=jnp.int32)
        o_a = diff_attn_block(q, k_all, v_all, q_pos, k_pos, lam, slopes)
    o_a = (rms_norm(o_a, prm['subln_g']) * (1.0 - lam_init)).reshape(B, T, W_A)
    o_b, S_new, shift_new = rwkv_branch(pr, prev, S0, prm)
    g_a, g_b = jnp.split(jax.nn.sigmoid(gates), 2, axis=-1)
    mixed = (g_a * o_a + g_b * o_b) @ prm['w_out']
    x = x + gt1 * mixed
    h2 = rms_norm(x, prm['norm_ffn_g']) * (1.0 + sc2) + sh2
    ffn = peer(h2.reshape(B * T, D_MODEL), prm['peer_w_query'], prm['peer_sub_keys'],
               prm['peer_u'], prm['peer_v']).reshape(B, T, D_MODEL)
    x = x + gt2 * ffn
    return x, (k.reshape(B, T, N_HEADS_A, 2 * HEAD_DIM_A), v, S_new, shift_new)


def setup_inputs(seed: int = 0) -> dict:
    key = jax.random.key(seed)
    ks = iter(jax.random.split(key, 40))
    f32 = jnp.float32
    nrm = lambda shape, scale: scale * jax.random.normal(next(ks), shape, f32)
    gain = lambda shape: 1.0 + nrm(shape, 0.01)
    uni = lambda shape, lo, hi: jax.random.uniform(next(ks), shape, f32, lo, hi)
    L = DEPTH
    return {
        "x_prompt": nrm((BATCH, SEQ, D_MODEL), 1.0),
        "x_sample": nrm((DEC_BATCH, DEC_SEQ, D_MODEL), 1.0),
        "c_prompt": nrm((BATCH, D_MODEL), 1.0),
        "c_sample": nrm((DEC_BATCH, D_MODEL), 1.0),
        "cache_attn_k": nrm((L, DEC_BATCH, PAST_LEN, N_HEADS_A, 2 * HEAD_DIM_A), 1.0),
        "cache_attn_v": nrm((L, DEC_BATCH, PAST_LEN, N_HEADS_A, 2 * HEAD_DIM_A), 1.0),
        "state_rwkv_wkv": nrm((L, DEC_BATCH, N_HEADS_B, HEAD_SIZE_B, HEAD_SIZE_B), 0.3),
        "state_rwkv_shift": nrm((L, DEC_BATCH, C_SHIFT), 1.0),
        "norm_mix_g": gain((L, D_MODEL)),
        "norm_ffn_g": gain((L, D_MODEL)),
        "w_ada": nrm((L, D_MODEL, 6 * D_MODEL), D_MODEL ** -0.5),
        "b_ada": nrm((L, 6 * D_MODEL), 0.01),
        "w_in": nrm((L, D_MODEL, D_IN), D_MODEL ** -0.5),
        "q_norm_g": gain((L, HEAD_DIM_A)),
        "k_norm_g": gain((L, HEAD_DIM_A)),
        "lambda_qk": nrm((L, 4, HEAD_DIM_A), 0.1),
        "subln_g": gain((L, 2 * HEAD_DIM_A)),
        "shift_mu": uni((L, C_SHIFT), 0.0, 1.0),
        "decay_w0": uni((L, W_B), -6.0, -1.0),
        "decay_lora_up": nrm((L, D_DECAY_LORA, W_B), 0.5 * D_DECAY_LORA ** -0.5),
        "iclr_a0": nrm((L, W_B), 0.1),
        "iclr_lora_up": nrm((L, D_AAA_LORA, W_B), D_AAA_LORA ** -0.5),
        "gate_lora_up": nrm((L, D_GATE_LORA, W_B), D_GATE_LORA ** -0.5),
        "k_k": 0.85 + nrm((L, W_B), 0.01),
        "k_a": gain((L, W_B)),
        "r_k": nrm((L, N_HEADS_B, HEAD_SIZE_B), 0.1),
        "lnx_g": gain((L, W_B)),
        "lnx_b": nrm((L, W_B), 0.01),
        "w_out": nrm((L, D_MODEL, D_MODEL), D_MODEL ** -0.5),
        "peer_w_query": nrm((L, D_MODEL, PEER_HEADS * D_KEY), D_MODEL ** -0.5),
        "peer_sub_keys": nrm((L, PEER_HEADS, 2, N_KEYS, D_KEY // 2), (D_KEY // 2) ** -0.5),
        "peer_u": nrm((L, N_EXPERTS, D_MODEL), D_MODEL ** -0.5),
        "peer_v": nrm((L, N_EXPERTS, D_MODEL), PEER_HEADS ** -0.5),
    }


def reference(x_prompt, x_sample, c_prompt, c_sample, cache_attn_k, cache_attn_v, state_rwkv_wkv,
              state_rwkv_shift, norm_mix_g, norm_ffn_g, w_ada, b_ada, w_in, q_norm_g, k_norm_g,
              lambda_qk, subln_g, shift_mu, decay_w0, decay_lora_up, iclr_a0, iclr_lora_up,
              gate_lora_up, k_k, k_a, r_k, lnx_g, lnx_b, w_out, peer_w_query, peer_sub_keys,
              peer_u, peer_v):
    slopes = alibi_slopes()
    st_p, st_s = [], []
    for l in range(DEPTH):
        prm = {
            'norm_mix_g': norm_mix_g[l], 'norm_ffn_g': norm_ffn_g[l], 'w_ada': w_ada[l], 'b_ada': b_ada[l],
            'w_in': w_in[l], 'q_norm_g': q_norm_g[l], 'k_norm_g': k_norm_g[l], 'lambda_qk': lambda_qk[l],
            'subln_g': subln_g[l], 'shift_mu': shift_mu[l], 'decay_w0': decay_w0[l],
            'decay_lora_up': decay_lora_up[l], 'iclr_a0': iclr_a0[l], 'iclr_lora_up': iclr_lora_up[l],
            'gate_lora_up': gate_lora_up[l], 'k_k': k_k[l], 'k_a': k_a[l], 'r_k': r_k[l],
            'lnx_g': lnx_g[l], 'lnx_b': lnx_b[l], 'w_out': w_out[l], 'peer_w_query': peer_w_query[l],
            'peer_sub_keys': peer_sub_keys[l], 'peer_u': peer_u[l], 'peer_v': peer_v[l],
        }
        lam_init = lambda_init(l)
        x_prompt, sp = trunk_layer(x_prompt, c_prompt, prm, lam_init, slopes, None)
        x_sample, ss = trunk_layer(x_sample, c_sample, prm, lam_init, slopes,
                                   (cache_attn_k[l], cache_attn_v[l], state_rwkv_wkv[l], state_rwkv_shift[l]))
        st_p.append(sp)
        st_s.append(ss)
    new_k_prompt = jnp.stack([s[0] for s in st_p])
    new_v_prompt = jnp.stack([s[1] for s in st_p])
    wkv_prompt = jnp.stack([s[2] for s in st_p])
    shift_prompt = jnp.stack([s[3] for s in st_p])
    new_k_sample = jnp.stack([s[0] for s in st_s])
    new_v_sample = jnp.stack([s[1] for s in st_s])
    wkv_sample = jnp.stack([s[2] for s in st_s])
    shift_sample = jnp.stack([s[3] for s in st_s])
    return (x_prompt, x_sample, new_k_prompt, new_v_prompt, wkv_prompt, shift_prompt,
            new_k_sample, new_v_sample, wkv_sample, shift_sample)
```

```python
import functools
import math

import jax
import jax.numpy as jnp
from jax import lax
from jax.experimental import pallas as pl
from jax.experimental.pallas import tpu as pltpu

F32 = jnp.float32
BF16 = jnp.bfloat16

D_MODEL = 1024
N_HEADS_A = 8
HEAD_DIM_A = 64
W_A = 1024
N_HEADS_B = 16
HEAD_SIZE_B = 64
W_B = 1024
C_SHIFT = 3328
D_IN = 8448
CHUNK = 64
GN_EPS = 64e-5
N_KEYS = 128
PEER_HEADS = 8
PEER_TOPK = 16
NEG = -1e30
LANES = 128
VMEM_LIMIT = 56 * 1024 * 1024

_NT = (((1,), (1,)), ((), ()))


def _cparams(sem):
    return pltpu.CompilerParams(dimension_semantics=sem, vmem_limit_bytes=VMEM_LIMIT)


def _dot(a, b):
    return jnp.dot(a, b, preferred_element_type=F32)


def _dot_nt(a, b):
    return lax.dot_general(a, b, _NT, preferred_element_type=F32)


def _ada_kernel(c_ref, w_ref, b_ref, o_ref):
    c = c_ref[...]
    s = c * jax.nn.sigmoid(c)
    o_ref[...] = _dot(s.astype(BF16), w_ref[...].astype(BF16)) + b_ref[...]


def _ada(c, w_ada, b_ada):
    n = c.shape[0]
    tn = 1024
    return pl.pallas_call(
        _ada_kernel,
        grid=(6 * D_MODEL // tn,),
        in_specs=[
            pl.BlockSpec((n, D_MODEL), lambda j: (0, 0)),
            pl.BlockSpec((D_MODEL, tn), lambda j: (0, j)),
            pl.BlockSpec((1, tn), lambda j: (0, j)),
        ],
        out_specs=pl.BlockSpec((n, tn), lambda j: (0, j)),
        out_shape=jax.ShapeDtypeStruct((n, 6 * D_MODEL), F32),
        compiler_params=_cparams(("arbitrary",)),
        name="ada",
    )(c, w_ada, b_ada.reshape(1, -1))


_IN_TN = 256
_Q_T, _K_T, _V_T, _PR_T, _G_T = 0, 4, 8, 12, 25
_N_T = D_IN // _IN_TN


def _in_kernel(x_ref, sc_ref, sh_ref, g_ref, w_ref, qg_ref, kg_ref, bd_ref,
               q_ref, k_ref, v_ref, pr_ref, gt_ref, h_scr):
    j = pl.program_id(1)

    @pl.when(j == 0)
    def _():
        x = x_ref[...]
        ms = jnp.mean(x * x, axis=-1, keepdims=True)
        y = x * lax.rsqrt(ms + 1e-6) * g_ref[...]
        h_scr[...] = (y * (1.0 + sc_ref[0]) + sh_ref[0]).astype(BF16)

    acc = _dot(h_scr[...], w_ref[...])

    def head_norm(a, g):
        ms = _dot((a * a).astype(BF16), bd_ref[...])
        return a * lax.rsqrt(ms + 1e-6) * g

    @pl.when(j < _K_T)
    def _():
        q_ref[...] = head_norm(acc, qg_ref[...]).astype(BF16)

    @pl.when((j >= _K_T) & (j < _V_T))
    def _():
        k_ref[...] = head_norm(acc, kg_ref[...])

    @pl.when((j >= _V_T) & (j < _PR_T))
    def _():
        v_ref[...] = acc

    @pl.when((j >= _PR_T) & (j < _G_T))
    def _():
        pr_ref[...] = acc

    @pl.when(j >= _G_T)
    def _():
        gt_ref[...] = jax.nn.sigmoid(acc).astype(BF16)


def _in_proj(x2d, sc, sh, norm_g, w_in_bf, q_g, k_g, tm, rows_per_mod):
    n = x2d.shape[0]
    tn = _IN_TN
    r = sc.shape[1]
    bd = jnp.kron(jnp.eye(tn // HEAD_DIM_A, dtype=F32), jnp.full((HEAD_DIM_A, HEAD_DIM_A), 1.0 / HEAD_DIM_A, F32)).astype(BF16)
    qg = jnp.tile(q_g.reshape(1, -1), (1, tn // HEAD_DIM_A))
    kg = jnp.tile(k_g.reshape(1, -1), (1, tn // HEAD_DIM_A))

    def seg(first, count):
        return lambda i, j: (i, jnp.clip(j - first, 0, count - 1))

    mod_spec = pl.BlockSpec((1, r, D_MODEL), lambda i, j: (i // rows_per_mod, 0, 0))
    return pl.pallas_call(
        _in_kernel,
        grid=(n // tm, _N_T),
        in_specs=[
            pl.BlockSpec((tm, D_MODEL), lambda i, j: (i, 0)),
            mod_spec, mod_spec,
            pl.BlockSpec((1, D_MODEL), lambda i, j: (0, 0)),
            pl.BlockSpec((D_MODEL, tn), lambda i, j: (0, j)),
            pl.BlockSpec((1, tn), lambda i, j: (0, 0)),
            pl.BlockSpec((1, tn), lambda i, j: (0, 0)),
            pl.BlockSpec((tn, tn), lambda i, j: (0, 0)),
        ],
        out_specs=[
            pl.BlockSpec((tm, tn), seg(_Q_T, 4)),
            pl.BlockSpec((tm, tn), seg(_K_T, 4)),
            pl.BlockSpec((tm, tn), seg(_V_T, 4)),
            pl.BlockSpec((tm, tn), seg(_PR_T, 13)),
            pl.BlockSpec((tm, tn), seg(_G_T, 8)),
        ],
        out_shape=[
            jax.ShapeDtypeStruct((n, W_A), BF16),
            jax.ShapeDtypeStruct((n, W_A), F32),
            jax.ShapeDtypeStruct((n, W_A), F32),
            jax.ShapeDtypeStruct((n, C_SHIFT), F32),
            jax.ShapeDtypeStruct((n, 2 * D_MODEL), BF16),
        ],
        scratch_shapes=[pltpu.VMEM((tm, D_MODEL), BF16)],
        compiler_params=_cparams(("parallel", "arbitrary")),
        name="in_proj",
    )(x2d, sc, sh, norm_g.reshape(1, -1), w_in_bf, qg, kg, bd)


_TQ = 256


def _lam(lq_ref, lam_init):
    lq = lq_ref[...]
    a = jnp.sum(lq[0:1] * lq[1:2], axis=-1, keepdims=True)
    b = jnp.sum(lq[2:3] * lq[3:4], axis=-1, keepdims=True)
    return jnp.exp(a) - jnp.exp(b) + lam_init


def _subln(o, sg_ref, lam_init):
    ms = jnp.mean(o * o, axis=-1, keepdims=True)
    return o * lax.rsqrt(ms + 1e-6) * sg_ref[...] * (1.0 - lam_init)


def _split_q(q):
    lane = lax.broadcasted_iota(jnp.int32, q.shape, 1)
    qs = q * jnp.asarray(HEAD_DIM_A ** -0.5, BF16)
    zero = jnp.zeros_like(qs)
    return jnp.where(lane < HEAD_DIM_A, qs, zero), jnp.where(lane >= HEAD_DIM_A, qs, zero)


def _attn_prompt_kernel(slopes_ref, q_ref, k_ref, v_ref, lq_ref, sg_ref, o_ref, kb, vb, *, lam_init):
    h = pl.program_id(1)
    i = pl.program_id(2)
    tq = _TQ

    @pl.when(i == 0)
    def _():
        kb[...] = k_ref[...].astype(BF16)
        vb[:, :LANES] = v_ref[...].astype(BF16)
        vb[:, LANES:] = jnp.ones((vb.shape[0], LANES), BF16)

    slope = slopes_ref[h]
    qc = _split_q(q_ref[...])
    iq = lax.broadcasted_iota(jnp.int32, (tq, tq), 0)
    ik = lax.broadcasted_iota(jnp.int32, (tq, tq), 1)
    rel = (ik - iq).astype(F32)
    bias_off = slope * rel
    visible = (ik // CHUNK) <= (iq // CHUNK)
    bias_diag = jnp.where(visible, -slope * jnp.abs(rel), NEG)

    def block(j, bias, cst, carry):
        kj = kb[pl.ds(pl.multiple_of(j * tq, tq), tq), :]
        vj = vb[pl.ds(pl.multiple_of(j * tq, tq), tq), :]
        out = []
        for c in range(2):
            m, acc = carry[c]
            s = _dot_nt(qc[c], kj) + bias
            m_new = jnp.maximum(m, jnp.max(s, axis=-1, keepdims=True) + cst)
            p = jnp.exp(s - (m_new - cst))
            acc = acc * jnp.exp(m - m_new) + _dot(p.astype(BF16), vj)
            out.append((m_new, acc))
        return tuple(out)

    init = tuple((jnp.full((tq, 1), NEG, F32), jnp.zeros((tq, 2 * LANES), F32)) for _ in range(2))

    def off_body(j, carry):
        cst = -slope * ((i - j) * tq).astype(F32)
        return block(j, bias_off, cst, carry)

    carry = lax.fori_loop(0, i, off_body, init)
    carry = block(i, bias_diag, jnp.zeros((), F32), carry)

    lam = _lam(lq_ref, lam_init)
    o = [acc[:, :LANES] / acc[:, LANES:LANES + 1] for (_, acc) in carry]
    o_ref[...] = _subln(o[0] - lam * o[1], sg_ref, lam_init)


def _attn_prompt(q, k, v, slopes, lambda_qk, subln_g, batch, seq, lam_init):
    n = batch * seq
    nq = seq // _TQ
    kern = functools.partial(_attn_prompt_kernel, lam_init=lam_init)
    return pl.pallas_call(
        kern,
        grid=(batch, N_HEADS_A, nq),
        in_specs=[
            pl.BlockSpec(memory_space=pltpu.SMEM),
            pl.BlockSpec((_TQ, LANES), lambda b, h, i: (b * nq + i, h)),
            pl.BlockSpec((seq, LANES), lambda b, h, i: (b, h)),
            pl.BlockSpec((seq, LANES), lambda b, h, i: (b, h)),
            pl.BlockSpec((4, HEAD_DIM_A), lambda b, h, i: (0, 0)),
            pl.BlockSpec((1, LANES), lambda b, h, i: (0, 0)),
        ],
        out_specs=pl.BlockSpec((_TQ, LANES), lambda b, h, i: (b * nq + i, h)),
        out_shape=jax.ShapeDtypeStruct((n, W_A), F32),
        scratch_shapes=[pltpu.VMEM((seq, LANES), BF16), pltpu.VMEM((seq, 2 * LANES), BF16)],
        compiler_params=_cparams(("parallel", "parallel", "arbitrary")),
        name="attn_prompt",
    )(slopes, q, k, v, lambda_qk, subln_g.reshape(1, -1))


def _attn_sample_kernel(slopes_ref, q_ref, kp_ref, vp_ref, kn_ref, vn_ref, lq_ref, sg_ref, o_ref, *, lam_init, past):
    h = pl.program_id(1)
    slope = slopes_ref[h]
    t = q_ref.shape[0]
    qc = _split_q(q_ref[...])
    kp = kp_ref[0].astype(BF16)
    vp = vp_ref[0].astype(BF16)
    kn = kn_ref[...].astype(BF16)
    vn = vn_ref[...].astype(BF16)

    def bias(nk, k0):
        qpos = past + lax.broadcasted_iota(jnp.int32, (t, nk), 0)
        kpos = k0 + lax.broadcasted_iota(jnp.int32, (t, nk), 1)
        vis = (kpos // CHUNK) <= (qpos // CHUNK)
        return jnp.where(vis, -slope * jnp.abs(qpos - kpos).astype(F32), NEG)

    bp = bias(past, 0)
    bn = bias(t, past)
    o = []
    for c in range(2):
        sp = _dot_nt(qc[c], kp) + bp
        sn = _dot_nt(qc[c], kn) + bn
        m = jnp.maximum(jnp.max(sp, axis=-1, keepdims=True), jnp.max(sn, axis=-1, keepdims=True))
        pp = jnp.exp(sp - m)
        pn = jnp.exp(sn - m)
        l = jnp.sum(pp, axis=-1, keepdims=True) + jnp.sum(pn, axis=-1, keepdims=True)
        o.append((_dot(pp.astype(BF16), vp) + _dot(pn.astype(BF16), vn)) / l)
    lam = _lam(lq_ref, lam_init)
    o_ref[...] = _subln(o[0] - lam * o[1], sg_ref, lam_init)


def _attn_sample(q, k_new, v_new, k_past, v_past, slopes, lambda_qk, subln_g, batch, t, lam_init):
    past = k_past.shape[1]
    kern = functools.partial(_attn_sample_kernel, lam_init=lam_init, past=past)
    return pl.pallas_call(
        kern,
        grid=(batch, N_HEADS_A),
        in_specs=[
            pl.BlockSpec(memory_space=pltpu.SMEM),
            pl.BlockSpec((t, LANES), lambda b, h: (b, h)),
            pl.BlockSpec((1, past, LANES), lambda b, h: (b, 0, h)),
            pl.BlockSpec((1, past, LANES), lambda b, h: (b, 0, h)),
            pl.BlockSpec((t, LANES), lambda b, h: (b, h)),
            pl.BlockSpec((t, LANES), lambda b, h: (b, h)),
            pl.BlockSpec((4, HEAD_DIM_A), lambda b, h: (0, 0)),
            pl.BlockSpec((1, LANES), lambda b, h: (0, 0)),
        ],
        out_specs=pl.BlockSpec((t, LANES), lambda b, h: (b, h)),
        out_shape=jax.ShapeDtypeStruct((batch * t, W_A), F32),
        compiler_params=_cparams(("parallel", "parallel")),
        name="attn_sample",
    )(slopes, q, k_past, v_past, k_new, v_new, lambda_qk, subln_g.reshape(1, -1))


_N_PAIR = N_HEADS_B // 2
_XW_COL = 3 * W_B
_XG_COL = 3 * W_B + LANES
_GRP = 8


def _rwkv_kernel(pr_ref, prev_ref, s0_ref, mu_ref, w0_ref, wd_ref, a0_ref, wa_ref, wg_ref, kk_ref, ka_ref, rk_ref,
                 lg_ref, lb_ref, bd_ref, ovt_ref, eye_ref,
                 ob_ref, sout_ref,
                 s_s, prev_s, r_s, w_s, k_s, kk_s, b_s, g_s, bon_s, vt_s, y_s):
    c = pl.program_id(1)
    tc = pr_ref.shape[0]

    @pl.when(c == 0)
    def _():
        prev_s[...] = prev_ref[0]
        s_s[...] = s0_ref[0]

    pr = pr_ref[...]
    row = lax.broadcasted_iota(jnp.int32, pr.shape, 0)
    shifted = jnp.where(row == 0, prev_s[...], pltpu.roll(pr, 1, 0))
    prev_s[...] = pr[tc - 1:tc, :]
    xs = pr + (shifted - pr) * mu_ref[...]
    r = xs[:, 0:W_B]
    k = xs[:, W_B:2 * W_B]
    v = xs[:, 2 * W_B:3 * W_B]
    xwa = xs[:, _XW_COL:_XW_COL + LANES]
    xg = xs[:, _XG_COL:_XG_COL + LANES]
    z = w0_ref[...] + _dot(jnp.tanh(xwa).astype(BF16), wd_ref[...])
    wlog = -(jnp.maximum(-z, 0.0) + jnp.log(1.0 + jnp.exp(-jnp.abs(z)))) - 0.5
    decay = jnp.exp(-jnp.exp(wlog))
    a = jax.nn.sigmoid(a0_ref[...] + _dot(xwa.astype(BF16), wa_ref[...]))
    g_s[...] = _dot(jax.nn.sigmoid(xg).astype(BF16), wg_ref[...])
    kkraw = k * kk_ref[...]
    k_mod = k * (1.0 + (a - 1.0) * ka_ref[...])
    rkr = r * k_mod * rk_ref[...]
    bd = bd_ref[...]
    for p in range(_N_PAIR):
        sl = slice(p * LANES, (p + 1) * LANES)
        kq = kkraw[:, sl]
        ss = _dot((kq * kq).astype(BF16), bd)
        kkn = kq / jnp.maximum(jnp.sqrt(ss), 1e-12)
        kk_s[:, sl] = kkn
        b_s[:, sl] = kkn * a[:, sl]
        bon_s[:, sl] = _dot(rkr[:, sl].astype(BF16), bd) * v[:, sl]
        vp = v[:, sl]
        lane = lax.broadcasted_iota(jnp.int32, vp.shape, 1)
        v2 = jnp.concatenate([jnp.where(lane < HEAD_SIZE_B, vp, 0.0), jnp.where(lane >= HEAD_SIZE_B, vp, 0.0)], axis=0)
        vt_s[p] = _dot_nt(eye_ref[...], v2.astype(BF16))
    r_s[...] = r
    w_s[...] = decay
    k_s[...] = k_mod

    ones_bd = bd
    ones_vt = ovt_ref[...]
    lane_t = lax.broadcasted_iota(jnp.int32, (HEAD_SIZE_B, 2 * tc), 1) % tc
    sub = lax.broadcasted_iota(jnp.int32, (2 * _GRP, LANES), 0)
    ln16 = lax.broadcasted_iota(jnp.int32, (2 * _GRP, LANES), 1)

    def group(gi, carry):
        base = pl.multiple_of(gi * _GRP, _GRP)
        for p in range(_N_PAIR):
            sl = slice(p * LANES, (p + 1) * LANES)
            kk8 = kk_s[pl.ds(base, _GRP), sl]
            w8 = w_s[pl.ds(base, _GRP), sl]
            b8 = b_s[pl.ds(base, _GRP), sl]
            k8 = k_s[pl.ds(base, _GRP), sl]
            r8 = r_s[pl.ds(base, _GRP), sl]
            s = s_s[p]
            vt = vt_s[p]
            yacc = jnp.zeros((2 * _GRP, HEAD_SIZE_B), F32)
            for i in range(_GRP):
                t = base + i
                sa = _dot((s * kk8[i:i + 1]).astype(BF16), ones_bd)
                vcol = _dot(jnp.where(lane_t == t, vt, 0.0).astype(BF16), ones_vt)
                s = s * w8[i:i + 1] - sa * b8[i:i + 1] + vcol * k8[i:i + 1]
                rsel = ((sub == i) & (ln16 < HEAD_SIZE_B)) | ((sub == _GRP + i) & (ln16 >= HEAD_SIZE_B))
                r2 = jnp.where(rsel, jnp.broadcast_to(r8[i:i + 1], (2 * _GRP, LANES)), 0.0).astype(BF16)
                yacc = yacc + _dot_nt(r2, s.astype(BF16))
            s_s[p] = s
            y_s[2 * p, pl.ds(base, _GRP), :] = yacc[:_GRP]
            y_s[2 * p + 1, pl.ds(base, _GRP), :] = yacc[_GRP:]
        return carry

    lax.fori_loop(0, tc // _GRP, group, 0)

    for h in range(N_HEADS_B):
        sl = slice(h * HEAD_SIZE_B, (h + 1) * HEAD_SIZE_B)
        y = y_s[h]
        m = jnp.mean(y, axis=-1, keepdims=True)
        var = jnp.mean(jnp.square(y - m), axis=-1, keepdims=True)
        yn = (y - m) * lax.rsqrt(var + GN_EPS) * lg_ref[h] + lb_ref[h]
        ob_ref[:, sl] = (yn + bon_s[:, sl]) * g_s[:, sl]

    @pl.when(c == pl.num_programs(1) - 1)
    def _():
        sout_ref[0] = s_s[...]


def _rwkv(pr, prev_row, s0_pairs, prm, batch, seq, tc):
    n = batch * seq
    nt = seq // tc
    row = lambda a: a.reshape(1, -1).astype(F32)
    zeros = jnp.zeros((HEAD_SIZE_B, W_B), F32)
    wd = jnp.concatenate([prm['decay_lora_up'], zeros], axis=0).astype(BF16)
    wa = jnp.concatenate([zeros, prm['iclr_lora_up']], axis=0).astype(BF16)
    wg = prm['gate_lora_up'].astype(BF16)
    bd = jnp.kron(jnp.eye(2, dtype=F32), jnp.ones((HEAD_SIZE_B, HEAD_SIZE_B), F32)).astype(BF16)
    ovt = jnp.kron(jnp.eye(2, dtype=F32), jnp.ones((tc, HEAD_SIZE_B), F32)).astype(BF16)
    eye2 = jnp.concatenate([jnp.eye(HEAD_SIZE_B, dtype=F32)] * 2, axis=1).astype(BF16)
    lg = prm['lnx_g'].reshape(N_HEADS_B, 1, HEAD_SIZE_B)
    lb = prm['lnx_b'].reshape(N_HEADS_B, 1, HEAD_SIZE_B)
    full = lambda shape: pl.BlockSpec(shape, lambda b, c: (0,) * len(shape))
    vec = full((1, W_B))
    return pl.pallas_call(
        _rwkv_kernel,
        grid=(batch, nt),
        in_specs=[
            pl.BlockSpec((tc, C_SHIFT), lambda b, c: (b * nt + c, 0)),
            pl.BlockSpec((1, 1, C_SHIFT), lambda b, c: (b, 0, 0)),
            pl.BlockSpec((1, _N_PAIR, HEAD_SIZE_B, LANES), lambda b, c: (b, 0, 0, 0)),
            full((1, C_SHIFT)), vec, full((LANES, W_B)), vec, full((LANES, W_B)), full((LANES, W_B)),
            vec, vec, vec,
            full((N_HEADS_B, 1, HEAD_SIZE_B)), full((N_HEADS_B, 1, HEAD_SIZE_B)),
            full((LANES, LANES)), full((2 * tc, LANES)), full((HEAD_SIZE_B, LANES)),
        ],
        out_specs=[
            pl.BlockSpec((tc, W_B), lambda b, c: (b * nt + c, 0)),
            pl.BlockSpec((1, _N_PAIR, HEAD_SIZE_B, LANES), lambda b, c: (b, 0, 0, 0)),
        ],
        out_shape=[
            jax.ShapeDtypeStruct((n, W_B), F32),
            jax.ShapeDtypeStruct((batch, _N_PAIR, HEAD_SIZE_B, LANES), F32),
        ],
        scratch_shapes=[
            pltpu.VMEM((_N_PAIR, HEAD_SIZE_B, LANES), F32),
            pltpu.VMEM((1, C_SHIFT), F32),
        ] + [pltpu.VMEM((tc, W_B), F32)] * 7 + [
            pltpu.VMEM((_N_PAIR, HEAD_SIZE_B, 2 * tc), F32),
            pltpu.VMEM((N_HEADS_B, tc, HEAD_SIZE_B), F32),
        ],
        compiler_params=_cparams(("parallel", "arbitrary")),
        name="rwkv",
    )(pr, prev_row, s0_pairs, row(prm['shift_mu']), row(prm['decay_w0']), wd, row(prm['iclr_a0']), wa, wg,
      row(prm['k_k']), row(prm['k_a']), row(prm['r_k']), lg, lb, bd, ovt, eye2)


def _merge_kernel(oa_ref, ob_ref, g_ref, x_ref, gt1_ref, sc2_ref, sh2_ref, ng_ref, w_ref, x1_ref, h2_ref):
    g = g_ref[...].astype(F32)
    mixed = (g[:, :D_MODEL] * oa_ref[...] + g[:, D_MODEL:] * ob_ref[...]).astype(BF16)
    x1 = x_ref[...] + gt1_ref[0] * _dot(mixed, w_ref[...])
    x1_ref[...] = x1
    ms = jnp.mean(x1 * x1, axis=-1, keepdims=True)
    y = x1 * lax.rsqrt(ms + 1e-6) * ng_ref[...]
    h2_ref[...] = (y * (1.0 + sc2_ref[0]) + sh2_ref[0]).astype(BF16)


def _merge(o_a, o_b, gates, x2d, gt1, sc2, sh2, norm_g, w_out_bf, tm, rows_per_mod):
    n = x2d.shape[0]
    r = gt1.shape[1]
    rowblk = pl.BlockSpec((tm, D_MODEL), lambda i: (i, 0))
    mod_spec = pl.BlockSpec((1, r, D_MODEL), lambda i: (i // rows_per_mod, 0, 0))
    return pl.pallas_call(
        _merge_kernel,
        grid=(n // tm,),
        in_specs=[
            rowblk, rowblk,
            pl.BlockSpec((tm, 2 * D_MODEL), lambda i: (i, 0)),
            rowblk, mod_spec, mod_spec, mod_spec,
            pl.BlockSpec((1, D_MODEL), lambda i: (0, 0)),
            pl.BlockSpec((D_MODEL, D_MODEL), lambda i: (0, 0)),
        ],
        out_specs=[rowblk, rowblk],
        out_shape=[jax.ShapeDtypeStruct((n, D_MODEL), F32), jax.ShapeDtypeStruct((n, D_MODEL), BF16)],
        compiler_params=_cparams(("parallel",)),
        name="merge",
    )(o_a, o_b, gates, x2d, gt1, sc2, sh2, norm_g.reshape(1, -1), w_out_bf)


_CAND = [(a, b) for a in range(PEER_TOPK) for b in range(PEER_TOPK) if (a + 1) * (b + 1) <= PEER_TOPK]


def _peer_prep_kernel(h2_ref, wq_ref, keys_ref, s2_ref, e2_ref, th_ref, cc_ref, s1_s, s2_s, v1_s, v2_s):
    h2 = h2_ref[...]
    ninf = -jnp.inf

    def head(h, carry):
        for c, (s_s, v_s) in enumerate(((s1_s, v1_s), (s2_s, v2_s))):
            qh = _dot(h2, wq_ref[2 * h + c]).astype(BF16)
            s = _dot_nt(keys_ref[2 * h + c], qh)
            s_s[h] = s
            cur = s
            for a in range(PEER_TOPK):
                m = jnp.max(cur, axis=0, keepdims=True)
                v_s[h, a] = m
                cur = jnp.where(cur == m, ninf, cur)
        return carry

    lax.fori_loop(0, PEER_HEADS, head, 0)

    v1 = [jnp.concatenate([v1_s[h, a] for h in range(PEER_HEADS)], axis=0) for a in range(PEER_TOPK)]
    v2 = [jnp.concatenate([v2_s[h, a] for h in range(PEER_HEADS)], axis=0) for a in range(PEER_TOPK)]
    cand = [v1[a] + v2[b] for (a, b) in _CAND]
    tops = []
    for _ in range(PEER_TOPK):
        m = cand[0]
        for cv in cand[1:]:
            m = jnp.maximum(m, cv)
        tops.append(m)
        rem = jnp.ones_like(m)
        nxt = []
        for cv in cand:
            hit = jnp.where(cv == m, rem, 0.0)
            nxt.append(jnp.where(hit > 0.0, ninf, cv))
            rem = rem - hit
        cand = nxt
    tau = tops[PEER_TOPK - 1]
    z = jnp.exp(tops[0] - tops[0])
    for a in range(1, PEER_TOPK):
        z = z + jnp.exp(tops[a] - tops[0])
    rz = 1.0 / z
    for h in range(PEER_HEADS):
        s1 = s1_s[h]
        s2 = s2_s[h]
        s2_ref[h] = s2
        e2_ref[h] = jnp.exp(s2 - v2[0][h:h + 1])
        th_ref[h] = tau[h:h + 1] - s1
        cc_ref[h] = jnp.exp(s1 - v1[0][h:h + 1]) * rz[h:h + 1]


def _peer_prep(h2, wq_heads, keys_bf, tb):
    n = h2.shape[0]
    out = jax.ShapeDtypeStruct((PEER_HEADS, N_KEYS, n), F32)
    ospec = pl.BlockSpec((PEER_HEADS, N_KEYS, tb), lambda i: (0, 0, i))
    return pl.pallas_call(
        _peer_prep_kernel,
        grid=(n // tb,),
        in_specs=[
            pl.BlockSpec((tb, D_MODEL), lambda i: (i, 0)),
            pl.BlockSpec((2 * PEER_HEADS, D_MODEL, N_KEYS), lambda i: (0, 0, 0)),
            pl.BlockSpec((2 * PEER_HEADS, N_KEYS, N_KEYS), lambda i: (0, 0, 0)),
        ],
        out_specs=[ospec] * 4,
        out_shape=[out] * 4,
        scratch_shapes=[
            pltpu.VMEM((PEER_HEADS, N_KEYS, tb), F32), pltpu.VMEM((PEER_HEADS, N_KEYS, tb), F32),
            pltpu.VMEM((PEER_HEADS, PEER_TOPK, 1, tb), F32), pltpu.VMEM((PEER_HEADS, PEER_TOPK, 1, tb), F32),
        ],
        compiler_params=_cparams(("parallel",)),
        name="peer_prep",
    )(h2, wq_heads, keys_bf)


_TE = 1024
_ROWS = _TE // N_KEYS


def _peer_dense_kernel(h2_ref, u_ref, vt_ref, s2_ref, e2_ref, th_ref, cc_ref, x1_ref, gt2_ref, o_ref, acc, p_s):
    e = pl.program_id(1)

    @pl.when(e == 0)
    def _():
        acc[...] = jnp.zeros_like(acc)

    ht = _dot_nt(u_ref[...], h2_ref[...])
    base = pl.multiple_of(e * _ROWS, _ROWS)
    th8 = [th_ref[h, pl.ds(base, _ROWS), :] for h in range(PEER_HEADS)]
    cc8 = [cc_ref[h, pl.ds(base, _ROWS), :] for h in range(PEER_HEADS)]
    for r in range(_ROWS):
        g = None
        for h in range(PEER_HEADS):
            w = jnp.where(s2_ref[h] >= th8[h][r:r + 1], e2_ref[h] * cc8[h][r:r + 1], 0.0)
            g = w if g is None else g + w
        x = ht[r * N_KEYS:(r + 1) * N_KEYS]
        gelu = 0.5 * x * (1.0 + lax.erf(x * (2.0 ** -0.5)))
        p_s[r * N_KEYS:(r + 1) * N_KEYS, :] = (g * gelu).astype(BF16)
    acc[...] += _dot(vt_ref[...], p_s[...])

    @pl.when(e == pl.num_programs(1) - 1)
    def _():
        o_ref[...] = x1_ref[...] + gt2_ref[0] * acc[...].T


def _peer_dense(h2, u_bf, vt_bf, s2, e2, th, cc, x1, gt2, tb, rows_per_mod):
    n = h2.shape[0]
    ne = u_bf.shape[0] // _TE
    r = gt2.shape[1]
    tok = pl.BlockSpec((PEER_HEADS, N_KEYS, tb), lambda i, e: (0, 0, i))
    rowblk = pl.BlockSpec((tb, D_MODEL), lambda i, e: (i, 0))
    return pl.pallas_call(
        _peer_dense_kernel,
        grid=(n // tb, ne),
        in_specs=[
            rowblk,
            pl.BlockSpec((_TE, D_MODEL), lambda i, e: (e, 0)),
            pl.BlockSpec((D_MODEL, _TE), lambda i, e: (0, e)),
            tok, tok, tok, tok,
            rowblk,
            pl.BlockSpec((1, r, D_MODEL), lambda i, e: (i // rows_per_mod, 0, 0)),
        ],
        out_specs=rowblk,
        out_shape=jax.ShapeDtypeStruct((n, D_MODEL), F32),
        scratch_shapes=[pltpu.VMEM((D_MODEL, tb), F32), pltpu.VMEM((_TE, tb), BF16)],
        compiler_params=_cparams(("parallel", "arbitrary")),
        name="peer_dense",
    )(h2, u_bf, vt_bf, s2, e2, th, cc, x1, gt2)


def _state_to_pairs(s):
    b = s.shape[0]
    return s.reshape(b, _N_PAIR, 2, HEAD_SIZE_B, HEAD_SIZE_B).transpose(0, 1, 3, 2, 4).reshape(b, _N_PAIR, HEAD_SIZE_B, LANES)


def _pairs_to_state(s):
    b = s.shape[0]
    return s.reshape(b, _N_PAIR, HEAD_SIZE_B, 2, HEAD_SIZE_B).transpose(0, 1, 3, 2, 4).reshape(b, N_HEADS_B, HEAD_SIZE_B, HEAD_SIZE_B)


def _stream(x, mods, w, prm, past, cfg):
    batch, t, _ = x.shape
    n = batch * t
    tm, rpm, tc, tb, rpm_tb = cfg
    sh1, sc1, gt1, sh2, sc2, gt2 = mods
    x2d = x.reshape(n, D_MODEL)
    q, k, v, pr, gates = _in_proj(x2d, sc1, sh1, prm['norm_mix_g'], w['w_in'], prm['q_norm_g'], prm['k_norm_g'], tm, rpm)
    if past is None:
        o_a = _attn_prompt(q, k, v, w['slopes'], prm['lambda_qk'], prm['subln_g'], batch, t, w['lam_init'])
        prev = jnp.zeros((batch, 1, C_SHIFT), F32)
        s0 = jnp.zeros((batch, _N_PAIR, HEAD_SIZE_B, LANES), F32)
    else:
        k_past, v_past, s_past, prev_row = past
        p = k_past.shape[1]
        o_a = _attn_sample(q, k, v, k_past.reshape(batch, p, W_A), v_past.reshape(batch, p, W_A), w['slopes'],
                           prm['lambda_qk'], prm['subln_g'], batch, t, w['lam_init'])
        prev = prev_row.reshape(batch, 1, C_SHIFT)
        s0 = _state_to_pairs(s_past)
    o_b, s_pairs = _rwkv(pr, prev, s0, prm, batch, t, tc)
    x1, h2 = _merge(o_a, o_b, gates, x2d, gt1, sc2, sh2, prm['norm_ffn_g'], w['w_out'], tm if tm <= 512 else 512,
                    rpm if tm <= 512 else rpm * (tm // 512))
    s2, e2, th, cc = _peer_prep(h2, w['wq'], w['keys'], tb)
    y = _peer_dense(h2, w['u'], w['vt'], s2, e2, th, cc, x1, gt2, tb, rpm_tb)
    new_k = k.reshape(batch, t, N_HEADS_A, 2 * HEAD_DIM_A)
    new_v = v.reshape(batch, t, N_HEADS_A, 2 * HEAD_DIM_A)
    shift = pr.reshape(batch, t, C_SHIFT)[:, -1]
    return y.reshape(batch, t, D_MODEL), (new_k, new_v, _pairs_to_state(s_pairs), shift)


def kernel(x_prompt, x_sample, c_prompt, c_sample, cache_attn_k, cache_attn_v, state_rwkv_wkv, state_rwkv_shift, norm_mix_g, norm_ffn_g, w_ada, b_ada, w_in, q_norm_g, k_norm_g, lambda_qk, subln_g, shift_mu, decay_w0, decay_lora_up, iclr_a0, iclr_lora_up, gate_lora_up, k_k, k_a, r_k, lnx_g, lnx_b, w_out, peer_w_query, peer_sub_keys, peer_u, peer_v):
    depth = w_in.shape[0]
    bp, tp, _ = x_prompt.shape
    bs, ts, _ = x_sample.shape
    slopes = jnp.asarray([2.0 ** (-8.0 * (h + 1) / N_HEADS_A) for h in range(N_HEADS_A)], F32)
    st_p, st_s = [], []
    for l in range(depth):
        prm = {
            'norm_mix_g': norm_mix_g[l], 'norm_ffn_g': norm_ffn_g[l], 'q_norm_g': q_norm_g[l], 'k_norm_g': k_norm_g[l],
            'lambda_qk': lambda_qk[l], 'subln_g': subln_g[l], 'shift_mu': shift_mu[l], 'decay_w0': decay_w0[l],
            'decay_lora_up': decay_lora_up[l], 'iclr_a0': iclr_a0[l], 'iclr_lora_up': iclr_lora_up[l],
            'gate_lora_up': gate_lora_up[l], 'k_k': k_k[l], 'k_a': k_a[l], 'r_k': r_k[l], 'lnx_g': lnx_g[l], 'lnx_b': lnx_b[l],
        }
        w = {
            'w_in': w_in[l].astype(BF16),
            'w_out': w_out[l].astype(BF16),
            'wq': peer_w_query[l].reshape(D_MODEL, 2 * PEER_HEADS, N_KEYS).transpose(1, 0, 2).astype(BF16),
            'keys': peer_sub_keys[l].reshape(2 * PEER_HEADS, N_KEYS, N_KEYS).astype(BF16),
            'u': peer_u[l].astype(BF16),
            'vt': peer_v[l].T.astype(BF16),
            'slopes': slopes,
            'lam_init': 0.8 - 0.6 * math.exp(-0.3 * l),
        }
        ada = _ada(jnp.concatenate([c_prompt, c_sample], axis=0), w_ada[l], b_ada[l])
        mods = jnp.split(ada, 6, axis=-1)
        mods_p = [m[:bp].reshape(bp, 1, D_MODEL) for m in mods]
        mods_s = [jnp.repeat(m[bp:], ts, axis=0).reshape(1, bs * ts, D_MODEL) for m in mods]
        x_prompt, sp = _stream(x_prompt, mods_p, w, prm, None, (1024, tp // 1024, CHUNK, 512, tp // 512))
        x_sample, ss = _stream(x_sample, mods_s, w, prm,
                               (cache_attn_k[l], cache_attn_v[l], state_rwkv_wkv[l], state_rwkv_shift[l]),
                               (bs * ts, 1, ts, bs * ts, 1))
        st_p.append(sp)
        st_s.append(ss)
    stack = lambda sts, i: jnp.stack([s[i] for s in sts])
    return (x_prompt, x_sample, stack(st_p, 0), stack(st_p, 1), stack(st_p, 2), stack(st_p, 3),
            stack(st_s, 0), stack(st_s, 1), stack(st_s, 2), stack(st_s, 3))
```

```python
import functools
import math

import jax
import jax.numpy as jnp
from jax import lax
from jax.experimental import pallas as pl
from jax.experimental.pallas import tpu as pltpu

F32 = jnp.float32
BF16 = jnp.bfloat16

D_MODEL = 1024
N_HEADS_A = 8
HEAD_DIM_A = 64
W_A = 1024
N_HEADS_B = 16
HEAD_SIZE_B = 64
W_B = 1024
C_SHIFT = 3328
D_IN = 8448
CHUNK = 64
GN_EPS = 64e-5
N_KEYS = 128
PEER_HEADS = 8
PEER_TOPK = 16
NEG = -1e30
LANES = 128
VMEM_LIMIT = 56 * 1024 * 1024

_NT = (((1,), (1,)), ((), ()))


def _cparams(sem):
    return pltpu.CompilerParams(dimension_semantics=sem, vmem_limit_bytes=VMEM_LIMIT)


def _dot(a, b):
    return jnp.dot(a, b, preferred_element_type=F32)


def _dot_nt(a, b):
    return lax.dot_general(a, b, _NT, preferred_element_type=F32)


def _ada_kernel(c_ref, w_ref, b_ref, o_ref):
    c = c_ref[...]
    s = c * jax.nn.sigmoid(c)
    o_ref[...] = _dot(s.astype(BF16), w_ref[...].astype(BF16)) + b_ref[...]


def _ada(c, w_ada, b_ada):
    n = c.shape[0]
    tn = 1024
    return pl.pallas_call(
        _ada_kernel,
        grid=(6 * D_MODEL // tn,),
        in_specs=[
            pl.BlockSpec((n, D_MODEL), lambda j: (0, 0)),
            pl.BlockSpec((D_MODEL, tn), lambda j: (0, j)),
            pl.BlockSpec((1, tn), lambda j: (0, j)),
        ],
        out_specs=pl.BlockSpec((n, tn), lambda j: (0, j)),
        out_shape=jax.ShapeDtypeStruct((n, 6 * D_MODEL), F32),
        compiler_params=_cparams(("arbitrary",)),
        name="ada",
    )(c, w_ada, b_ada.reshape(1, -1))


_IN_TN = 256
_Q_T, _K_T, _V_T, _PR_T, _G_T = 0, 4, 8, 12, 25
_N_T = D_IN // _IN_TN


def _in_kernel(x_ref, sc_ref, sh_ref, g_ref, w_ref, qg_ref, kg_ref, bd_ref,
               q_ref, k_ref, v_ref, pr_ref, gt_ref, h_scr):
    j = pl.program_id(1)

    @pl.when(j == 0)
    def _():
        x = x_ref[...]
        ms = jnp.mean(x * x, axis=-1, keepdims=True)
        y = x * lax.rsqrt(ms + 1e-6) * g_ref[...]
        h_scr[...] = (y * (1.0 + sc_ref[0]) + sh_ref[0]).astype(BF16)

    acc = _dot(h_scr[...], w_ref[...])

    def head_norm(a, g):
        ms = _dot((a * a).astype(BF16), bd_ref[...])
        return a * lax.rsqrt(ms + 1e-6) * g

    @pl.when(j < _K_T)
    def _():
        q_ref[...] = head_norm(acc, qg_ref[...]).astype(BF16)

    @pl.when((j >= _K_T) & (j < _V_T))
    def _():
        k_ref[...] = head_norm(acc, kg_ref[...])

    @pl.when((j >= _V_T) & (j < _PR_T))
    def _():
        v_ref[...] = acc

    @pl.when((j >= _PR_T) & (j < _G_T))
    def _():
        pr_ref[...] = acc

    @pl.when(j >= _G_T)
    def _():
        gt_ref[...] = jax.nn.sigmoid(acc).astype(BF16)


def _in_proj(x2d, sc, sh, norm_g, w_in_bf, q_g, k_g, tm, rows_per_mod):
    n = x2d.shape[0]
    tn = _IN_TN
    r = sc.shape[1]
    bd = jnp.kron(jnp.eye(tn // HEAD_DIM_A, dtype=F32), jnp.full((HEAD_DIM_A, HEAD_DIM_A), 1.0 / HEAD_DIM_A, F32)).astype(BF16)
    qg = jnp.tile(q_g.reshape(1, -1), (1, tn // HEAD_DIM_A))
    kg = jnp.tile(k_g.reshape(1, -1), (1, tn // HEAD_DIM_A))

    def seg(first, count):
        return lambda i, j: (i, jnp.clip(j - first, 0, count - 1))

    mod_spec = pl.BlockSpec((1, r, D_MODEL), lambda i, j: (i // rows_per_mod, 0, 0))
    return pl.pallas_call(
        _in_kernel,
        grid=(n // tm, _N_T),
        in_specs=[
            pl.BlockSpec((tm, D_MODEL), lambda i, j: (i, 0)),
            mod_spec, mod_spec,
            pl.BlockSpec((1, D_MODEL), lambda i, j: (0, 0)),
            pl.BlockSpec((D_MODEL, tn), lambda i, j: (0, j)),
            pl.BlockSpec((1, tn), lambda i, j: (0, 0)),
            pl.BlockSpec((1, tn), lambda i, j: (0, 0)),
            pl.BlockSpec((tn, tn), lambda i, j: (0, 0)),
        ],
        out_specs=[
            pl.BlockSpec((tm, tn), seg(_Q_T, 4)),
            pl.BlockSpec((tm, tn), seg(_K_T, 4)),
            pl.BlockSpec((tm, tn), seg(_V_T, 4)),
            pl.BlockSpec((tm, tn), seg(_PR_T, 13)),
            pl.BlockSpec((tm, tn), seg(_G_T, 8)),
        ],
        out_shape=[
            jax.ShapeDtypeStruct((n, W_A), BF16),
            jax.ShapeDtypeStruct((n, W_A), F32),
            jax.ShapeDtypeStruct((n, W_A), F32),
            jax.ShapeDtypeStruct((n, C_SHIFT), F32),
            jax.ShapeDtypeStruct((n, 2 * D_MODEL), BF16),
        ],
        scratch_shapes=[pltpu.VMEM((tm, D_MODEL), BF16)],
        compiler_params=_cparams(("parallel", "arbitrary")),
        name="in_proj",
    )(x2d, sc, sh, norm_g.reshape(1, -1), w_in_bf, qg, kg, bd)


_TQ = 512


def _lam(lq_ref, lam_init):
    lq = lq_ref[...]
    a = jnp.sum(lq[0:1] * lq[1:2], axis=-1, keepdims=True)
    b = jnp.sum(lq[2:3] * lq[3:4], axis=-1, keepdims=True)
    return jnp.exp(a) - jnp.exp(b) + lam_init


def _subln(o, sg_ref, lam_init):
    ms = jnp.mean(o * o, axis=-1, keepdims=True)
    return o * lax.rsqrt(ms + 1e-6) * sg_ref[...] * (1.0 - lam_init)


def _split_q(q):
    lane = lax.broadcasted_iota(jnp.int32, q.shape, 1)
    qs = q * jnp.asarray(HEAD_DIM_A ** -0.5, BF16)
    zero = jnp.zeros_like(qs)
    return jnp.where(lane < HEAD_DIM_A, qs, zero), jnp.where(lane >= HEAD_DIM_A, qs, zero)


def _attn_prompt_kernel(slopes_ref, q_ref, k_ref, v_ref, lq_ref, sg_ref, o_ref, kb, vb, *, lam_init):
    h = pl.program_id(1)
    i = pl.program_id(2)
    tq = _TQ

    @pl.when(i == 0)
    def _():
        kb[...] = k_ref[...].astype(BF16)
        vb[:, :LANES] = v_ref[...].astype(BF16)
        vb[:, LANES:] = jnp.ones((vb.shape[0], LANES), BF16)

    slope = slopes_ref[h]
    qc = _split_q(q_ref[...])
    iq = lax.broadcasted_iota(jnp.int32, (tq, tq), 0)
    ik = lax.broadcasted_iota(jnp.int32, (tq, tq), 1)
    rel = (ik - iq).astype(F32)
    bias_off = slope * rel
    visible = (ik // CHUNK) <= (iq // CHUNK)
    bias_diag = jnp.where(visible, -slope * jnp.abs(rel), NEG)

    def block(j, bias, cst, carry):
        kj = kb[pl.ds(pl.multiple_of(j * tq, tq), tq), :]
        vj = vb[pl.ds(pl.multiple_of(j * tq, tq), tq), :]
        out = []
        for c in range(2):
            m, acc = carry[c]
            s = _dot_nt(qc[c], kj) + bias
            m_new = jnp.maximum(m, jnp.max(s, axis=-1, keepdims=True) + cst)
            p = jnp.exp(s - (m_new - cst))
            acc = acc * jnp.exp(m - m_new) + _dot(p.astype(BF16), vj)
            out.append((m_new, acc))
        return tuple(out)

    init = tuple((jnp.full((tq, 1), NEG, F32), jnp.zeros((tq, 2 * LANES), F32)) for _ in range(2))

    def off_body(j, carry):
        cst = -slope * ((i - j) * tq).astype(F32)
        return block(j, bias_off, cst, carry)

    carry = lax.fori_loop(0, i, off_body, init)
    carry = block(i, bias_diag, jnp.zeros((), F32), carry)

    lam = _lam(lq_ref, lam_init)
    o = [acc[:, :LANES] / acc[:, LANES:LANES + 1] for (_, acc) in carry]
    o_ref[...] = _subln(o[0] - lam * o[1], sg_ref, lam_init)


def _attn_prompt(q, k, v, slopes, lambda_qk, subln_g, batch, seq, lam_init):
    n = batch * seq
    nq = seq // _TQ
    kern = functools.partial(_attn_prompt_kernel, lam_init=lam_init)
    return pl.pallas_call(
        kern,
        grid=(batch, N_HEADS_A, nq),
        in_specs=[
            pl.BlockSpec(memory_space=pltpu.SMEM),
            pl.BlockSpec((_TQ, LANES), lambda b, h, i: (b * nq + i, h)),
            pl.BlockSpec((seq, LANES), lambda b, h, i: (b, h)),
            pl.BlockSpec((seq, LANES), lambda b, h, i: (b, h)),
            pl.BlockSpec((4, HEAD_DIM_A), lambda b, h, i: (0, 0)),
            pl.BlockSpec((1, LANES), lambda b, h, i: (0, 0)),
        ],
        out_specs=pl.BlockSpec((_TQ, LANES), lambda b, h, i: (b * nq + i, h)),
        out_shape=jax.ShapeDtypeStruct((n, W_A), F32),
        scratch_shapes=[pltpu.VMEM((seq, LANES), BF16), pltpu.VMEM((seq, 2 * LANES), BF16)],
        compiler_params=_cparams(("parallel", "parallel", "arbitrary")),
        name="attn_prompt",
    )(slopes, q, k, v, lambda_qk, subln_g.reshape(1, -1))


def _attn_sample_kernel(slopes_ref, q_ref, kp_ref, vp_ref, kn_ref, vn_ref, lq_ref, sg_ref, o_ref, *, lam_init, past):
    h = pl.program_id(1)
    slope = slopes_ref[h]
    t = q_ref.shape[0]
    qc = _split_q(q_ref[...])
    kp = kp_ref[0].astype(BF16)
    vp = vp_ref[0].astype(BF16)
    kn = kn_ref[...].astype(BF16)
    vn = vn_ref[...].astype(BF16)

    def bias(nk, k0):
        qpos = past + lax.broadcasted_iota(jnp.int32, (t, nk), 0)
        kpos = k0 + lax.broadcasted_iota(jnp.int32, (t, nk), 1)
        vis = (kpos // CHUNK) <= (qpos // CHUNK)
        return jnp.where(vis, -slope * jnp.abs(qpos - kpos).astype(F32), NEG)

    bp = bias(past, 0)
    bn = bias(t, past)
    o = []
    for c in range(2):
        sp = _dot_nt(qc[c], kp) + bp
        sn = _dot_nt(qc[c], kn) + bn
        m = jnp.maximum(jnp.max(sp, axis=-1, keepdims=True), jnp.max(sn, axis=-1, keepdims=True))
        pp = jnp.exp(sp - m)
        pn = jnp.exp(sn - m)
        l = jnp.sum(pp, axis=-1, keepdims=True) + jnp.sum(pn, axis=-1, keepdims=True)
        o.append((_dot(pp.astype(BF16), vp) + _dot(pn.astype(BF16), vn)) / l)
    lam = _lam(lq_ref, lam_init)
    o_ref[...] = _subln(o[0] - lam * o[1], sg_ref, lam_init)


def _attn_sample(q, k_new, v_new, k_past, v_past, slopes, lambda_qk, subln_g, batch, t, lam_init):
    past = k_past.shape[1]
    kern = functools.partial(_attn_sample_kernel, lam_init=lam_init, past=past)
    return pl.pallas_call(
        kern,
        grid=(batch, N_HEADS_A),
        in_specs=[
            pl.BlockSpec(memory_space=pltpu.SMEM),
            pl.BlockSpec((t, LANES), lambda b, h: (b, h)),
            pl.BlockSpec((1, past, LANES), lambda b, h: (b, 0, h)),
            pl.BlockSpec((1, past, LANES), lambda b, h: (b, 0, h)),
            pl.BlockSpec((t, LANES), lambda b, h: (b, h)),
            pl.BlockSpec((t, LANES), lambda b, h: (b, h)),
            pl.BlockSpec((4, HEAD_DIM_A), lambda b, h: (0, 0)),
            pl.BlockSpec((1, LANES), lambda b, h: (0, 0)),
        ],
        out_specs=pl.BlockSpec((t, LANES), lambda b, h: (b, h)),
        out_shape=jax.ShapeDtypeStruct((batch * t, W_A), F32),
        compiler_params=_cparams(("parallel", "parallel")),
        name="attn_sample",
    )(slopes, q, k_past, v_past, k_new, v_new, lambda_qk, subln_g.reshape(1, -1))


_N_PAIR = N_HEADS_B // 2
_XW_COL = 3 * W_B
_XG_COL = 3 * W_B + LANES
_GRP = 8


def _rwkv_kernel(pr_ref, prev_ref, s0_ref, mu_ref, w0_ref, wd_ref, a0_ref, wa_ref, wg_ref, kk_ref, ka_ref, rk_ref,
                 lg_ref, lb_ref, bd_ref, o4_ref, eye_ref, sel_ref,
                 ob_ref, sout_ref,
                 s_s, prev_s, r_s, w_s, k_s, kk_s, b_s, g_s, bon_s, vt_s, yb_s):
    c = pl.program_id(1)
    nb, tc, _ = pr_ref.shape
    n_pair = nb * _N_PAIR

    @pl.when(c == 0)
    def _():
        prev_s[...] = prev_ref[...]
        s_s[...] = s0_ref[...].reshape(s_s.shape)

    bd = bd_ref[...]
    pad = HEAD_SIZE_B - tc
    for bb in range(nb):
        pr = pr_ref[bb]
        row = lax.broadcasted_iota(jnp.int32, pr.shape, 0)
        shifted = jnp.where(row == 0, prev_s[bb], pltpu.roll(pr, 1, 0))
        prev_s[bb] = pr[tc - 1:tc, :]
        xs = pr + (shifted - pr) * mu_ref[...]
        r = xs[:, 0:W_B]
        k = xs[:, W_B:2 * W_B]
        v = xs[:, 2 * W_B:3 * W_B]
        xwa = xs[:, _XW_COL:_XW_COL + LANES]
        xg = xs[:, _XG_COL:_XG_COL + LANES]
        z = w0_ref[...] + _dot(jnp.tanh(xwa).astype(BF16), wd_ref[...])
        wlog = -(jnp.maximum(-z, 0.0) + jnp.log(1.0 + jnp.exp(-jnp.abs(z)))) - 0.5
        decay = jnp.exp(-jnp.exp(wlog))
        a = jax.nn.sigmoid(a0_ref[...] + _dot(xwa.astype(BF16), wa_ref[...]))
        g_s[bb] = _dot(jax.nn.sigmoid(xg).astype(BF16), wg_ref[...])
        kkraw = k * kk_ref[...]
        k_mod = k * (1.0 + (a - 1.0) * ka_ref[...])
        rkr = r * k_mod * rk_ref[...]
        for p in range(_N_PAIR):
            sl = slice(p * LANES, (p + 1) * LANES)
            kq = kkraw[:, sl]
            ss = _dot((kq * kq).astype(BF16), bd)
            kkn = kq / jnp.maximum(jnp.sqrt(ss), 1e-12)
            kk_s[bb, :, sl] = kkn
            b_s[bb, :, sl] = kkn * a[:, sl]
            bon_s[bb, :, sl] = _dot(rkr[:, sl].astype(BF16), bd) * v[:, sl]
            vp = v[:, sl]
            lane = lax.broadcasted_iota(jnp.int32, vp.shape, 1)
            parts = []
            for hh in range(2):
                parts.append(jnp.where((lane >= HEAD_SIZE_B) == (hh == 1), vp, 0.0))
                if pad:
                    parts.append(jnp.zeros((pad, LANES), F32))
            vt_s[bb * _N_PAIR + p] = _dot_nt(eye_ref[...], jnp.concatenate(parts, axis=0).astype(BF16))
        r_s[bb] = r
        w_s[bb] = decay
        k_s[bb] = k_mod
    yb_s[...] = jnp.zeros_like(yb_s)

    ones4 = o4_ref[...]
    lane_t = lax.broadcasted_iota(jnp.int32, (HEAD_SIZE_B, LANES), 1) % HEAD_SIZE_B
    sls = [slice(p * LANES, (p + 1) * LANES) for p in range(_N_PAIR)]
    n_half = _N_PAIR // 2

    def stack(mats):
        return jnp.concatenate([jnp.concatenate([mats[2 * q], mats[2 * q + 1]], axis=1) for q in range(len(mats) // 2)], axis=0)

    def unstack(m, j):
        q, o = divmod(j, 2)
        return m[q * HEAD_SIZE_B:(q + 1) * HEAD_SIZE_B, o * LANES:(o + 1) * LANES]

    def group(gi, carry):
        base = pl.multiple_of(gi * _GRP, _GRP)
        rows = lambda ref: [ref[bb, pl.ds(base, _GRP), :] for bb in range(nb)]
        kk8, w8, b8, k8, r8 = rows(kk_s), rows(w_s), rows(b_s), rows(k_s), rows(r_s)
        row = lambda tiles, p, i: tiles[p // _N_PAIR][i:i + 1, sls[p % _N_PAIR]]
        s = [s_s[p] for p in range(n_pair)]
        for i in range(_GRP):
            tmask = lane_t == (base + i)
            vcol = []
            for bb in range(nb):
                vc = _dot(stack([jnp.where(tmask, vt_s[bb * _N_PAIR + p], 0.0).astype(BF16) for p in range(_N_PAIR)]), ones4)
                vcol += [unstack(vc, p) for p in range(_N_PAIR)]
            for half in range(2 * nb):
                ps = list(range(half * n_half, (half + 1) * n_half))
                sa = _dot(stack([(s[p] * row(kk8, p, i)).astype(BF16) for p in ps]), ones4)
                for j, p in enumerate(ps):
                    s[p] = s[p] * row(w8, p, i) - unstack(sa, j) * row(b8, p, i) + vcol[p] * row(k8, p, i)
            for bb in range(nb):
                ps = list(range(bb * _N_PAIR, (bb + 1) * _N_PAIR))
                yb = _dot(stack([(s[p] * row(r8, p, i)).astype(BF16) for p in ps]), ones4)
                for j, p in enumerate(ps):
                    yb_s[p] = jnp.where(tmask, unstack(yb, j), yb_s[p])
        for p in range(n_pair):
            s_s[p] = s[p]
        return carry

    lax.fori_loop(0, tc // _GRP, group, 0)

    for bb in range(nb):
        for h in range(N_HEADS_B):
            sl = slice(h * HEAD_SIZE_B, (h + 1) * HEAD_SIZE_B)
            ybp = yb_s[bb * _N_PAIR + h // 2]
            hi = ybp.astype(BF16)
            lo = (ybp - hi.astype(F32)).astype(BF16)
            sel = sel_ref[h % 2][:tc]
            y = _dot_nt(sel, hi) + _dot_nt(sel, lo)
            m = jnp.mean(y, axis=-1, keepdims=True)
            var = jnp.mean(jnp.square(y - m), axis=-1, keepdims=True)
            yn = (y - m) * lax.rsqrt(var + GN_EPS) * lg_ref[h] + lb_ref[h]
            ob_ref[bb, :, sl] = (yn + bon_s[bb, :, sl]) * g_s[bb, :, sl]

    @pl.when(c == pl.num_programs(1) - 1)
    def _():
        sout_ref[...] = s_s[...].reshape(sout_ref.shape)


_RWKV_NB = 2


def _rwkv(pr, prev_row, s0_pairs, prm, batch, seq, tc):
    n = batch * seq
    nt = seq // tc
    nb = _RWKV_NB
    row = lambda a: a.reshape(1, -1).astype(F32)
    zeros = jnp.zeros((HEAD_SIZE_B, W_B), F32)
    wd = jnp.concatenate([prm['decay_lora_up'], zeros], axis=0).astype(BF16)
    wa = jnp.concatenate([zeros, prm['iclr_lora_up']], axis=0).astype(BF16)
    wg = prm['gate_lora_up'].astype(BF16)
    ones_h = jnp.ones((HEAD_SIZE_B, HEAD_SIZE_B), F32)
    bd = jnp.kron(jnp.eye(2, dtype=F32), ones_h).astype(BF16)
    ones4 = jnp.kron(jnp.eye(4, dtype=F32), ones_h).astype(BF16)
    eye2 = jnp.concatenate([jnp.eye(HEAD_SIZE_B, dtype=F32)] * 2, axis=1).astype(BF16)
    sel = jnp.eye(LANES, dtype=F32).reshape(2, HEAD_SIZE_B, LANES).astype(BF16)
    lg = prm['lnx_g'].reshape(N_HEADS_B, 1, HEAD_SIZE_B)
    lb = prm['lnx_b'].reshape(N_HEADS_B, 1, HEAD_SIZE_B)
    full = lambda shape: pl.BlockSpec(shape, lambda b, c: (0,) * len(shape))
    vec = full((1, W_B))
    state_spec = pl.BlockSpec((nb, _N_PAIR, HEAD_SIZE_B, LANES), lambda b, c: (b, 0, 0, 0))
    o_b, s_new = pl.pallas_call(
        _rwkv_kernel,
        grid=(batch // nb, nt),
        in_specs=[
            pl.BlockSpec((nb, tc, C_SHIFT), lambda b, c: (b, c, 0)),
            pl.BlockSpec((nb, 1, C_SHIFT), lambda b, c: (b, 0, 0)),
            state_spec,
            full((1, C_SHIFT)), vec, full((LANES, W_B)), vec, full((LANES, W_B)), full((LANES, W_B)),
            vec, vec, vec,
            full((N_HEADS_B, 1, HEAD_SIZE_B)), full((N_HEADS_B, 1, HEAD_SIZE_B)),
            full((LANES, LANES)), full((2 * LANES, 2 * LANES)), full((HEAD_SIZE_B, LANES)), full((2, HEAD_SIZE_B, LANES)),
        ],
        out_specs=[
            pl.BlockSpec((nb, tc, W_B), lambda b, c: (b, c, 0)),
            state_spec,
        ],
        out_shape=[
            jax.ShapeDtypeStruct((batch, seq, W_B), F32),
            jax.ShapeDtypeStruct((batch, _N_PAIR, HEAD_SIZE_B, LANES), F32),
        ],
        scratch_shapes=[
            pltpu.VMEM((nb * _N_PAIR, HEAD_SIZE_B, LANES), F32),
            pltpu.VMEM((nb, 1, C_SHIFT), F32),
        ] + [pltpu.VMEM((nb, tc, W_B), F32)] * 7 + [
            pltpu.VMEM((nb * _N_PAIR, HEAD_SIZE_B, LANES), F32),
            pltpu.VMEM((nb * _N_PAIR, HEAD_SIZE_B, LANES), F32),
        ],
        compiler_params=_cparams(("parallel", "arbitrary")),
        name="rwkv",
    )(pr.reshape(batch, seq, C_SHIFT), prev_row, s0_pairs, row(prm['shift_mu']), row(prm['decay_w0']), wd, row(prm['iclr_a0']),
      wa, wg, row(prm['k_k']), row(prm['k_a']), row(prm['r_k']), lg, lb, bd, ones4, eye2, sel)
    return o_b.reshape(n, W_B), s_new


def _merge_kernel(oa_ref, ob_ref, g_ref, x_ref, gt1_ref, sc2_ref, sh2_ref, ng_ref, w_ref, x1_ref, h2_ref):
    g = g_ref[...].astype(F32)
    mixed = (g[:, :D_MODEL] * oa_ref[...] + g[:, D_MODEL:] * ob_ref[...]).astype(BF16)
    x1 = x_ref[...] + gt1_ref[0] * _dot(mixed, w_ref[...])
    x1_ref[...] = x1
    ms = jnp.mean(x1 * x1, axis=-1, keepdims=True)
    y = x1 * lax.rsqrt(ms + 1e-6) * ng_ref[...]
    h2_ref[...] = (y * (1.0 + sc2_ref[0]) + sh2_ref[0]).astype(BF16)


def _merge(o_a, o_b, gates, x2d, gt1, sc2, sh2, norm_g, w_out_bf, tm, rows_per_mod):
    n = x2d.shape[0]
    r = gt1.shape[1]
    rowblk = pl.BlockSpec((tm, D_MODEL), lambda i: (i, 0))
    mod_spec = pl.BlockSpec((1, r, D_MODEL), lambda i: (i // rows_per_mod, 0, 0))
    return pl.pallas_call(
        _merge_kernel,
        grid=(n // tm,),
        in_specs=[
            rowblk, rowblk,
            pl.BlockSpec((tm, 2 * D_MODEL), lambda i: (i, 0)),
            rowblk, mod_spec, mod_spec, mod_spec,
            pl.BlockSpec((1, D_MODEL), lambda i: (0, 0)),
            pl.BlockSpec((D_MODEL, D_MODEL), lambda i: (0, 0)),
        ],
        out_specs=[rowblk, rowblk],
        out_shape=[jax.ShapeDtypeStruct((n, D_MODEL), F32), jax.ShapeDtypeStruct((n, D_MODEL), BF16)],
        compiler_params=_cparams(("parallel",)),
        name="merge",
    )(o_a, o_b, gates, x2d, gt1, sc2, sh2, norm_g.reshape(1, -1), w_out_bf)


_CAND = [(a, b) for a in range(PEER_TOPK) for b in range(PEER_TOPK) if (a + 1) * (b + 1) <= PEER_TOPK]


def _peer_prep_kernel(h2_ref, wq_ref, keys_ref, s2_ref, e2_ref, th_ref, cc_ref, s1_s, s2_s, v1_s, v2_s):
    h2 = h2_ref[...]
    ninf = -jnp.inf

    def head(h, carry):
        for c, (s_s, v_s) in enumerate(((s1_s, v1_s), (s2_s, v2_s))):
            qh = _dot(h2, wq_ref[2 * h + c]).astype(BF16)
            s = _dot_nt(keys_ref[2 * h + c], qh)
            s_s[h] = s
            cur = s
            for a in range(PEER_TOPK):
                m = jnp.max(cur, axis=0, keepdims=True)
                v_s[h, a] = m
                cur = jnp.where(cur == m, ninf, cur)
        return carry

    lax.fori_loop(0, PEER_HEADS, head, 0)

    v1 = [jnp.concatenate([v1_s[h, a] for h in range(PEER_HEADS)], axis=0) for a in range(PEER_TOPK)]
    v2 = [jnp.concatenate([v2_s[h, a] for h in range(PEER_HEADS)], axis=0) for a in range(PEER_TOPK)]
    cand = [v1[a] + v2[b] for (a, b) in _CAND]
    tops = []
    for _ in range(PEER_TOPK):
        m = cand[0]
        for cv in cand[1:]:
            m = jnp.maximum(m, cv)
        tops.append(m)
        rem = jnp.ones_like(m)
        nxt = []
        for cv in cand:
            hit = jnp.where(cv == m, rem, 0.0)
            nxt.append(jnp.where(hit > 0.0, ninf, cv))
            rem = rem - hit
        cand = nxt
    tau = tops[PEER_TOPK - 1]
    z = jnp.exp(tops[0] - tops[0])
    for a in range(1, PEER_TOPK):
        z = z + jnp.exp(tops[a] - tops[0])
    rz = 1.0 / z
    for h in range(PEER_HEADS):
        s1 = s1_s[h]
        s2 = s2_s[h]
        s2_ref[h] = s2
        e2_ref[h] = jnp.exp(s2 - v2[0][h:h + 1])
        th_ref[h] = tau[h:h + 1] - s1
        cc_ref[h] = jnp.exp(s1 - v1[0][h:h + 1]) * rz[h:h + 1]


def _peer_prep(h2, wq_heads, keys_bf, tb):
    n = h2.shape[0]
    out = jax.ShapeDtypeStruct((PEER_HEADS, N_KEYS, n), F32)
    ospec = pl.BlockSpec((PEER_HEADS, N_KEYS, tb), lambda i: (0, 0, i))
    return pl.pallas_call(
        _peer_prep_kernel,
        grid=(n // tb,),
        in_specs=[
            pl.BlockSpec((tb, D_MODEL), lambda i: (i, 0)),
            pl.BlockSpec((2 * PEER_HEADS, D_MODEL, N_KEYS), lambda i: (0, 0, 0)),
            pl.BlockSpec((2 * PEER_HEADS, N_KEYS, N_KEYS), lambda i: (0, 0, 0)),
        ],
        out_specs=[ospec] * 4,
        out_shape=[out] * 4,
        scratch_shapes=[
            pltpu.VMEM((PEER_HEADS, N_KEYS, tb), F32), pltpu.VMEM((PEER_HEADS, N_KEYS, tb), F32),
            pltpu.VMEM((PEER_HEADS, PEER_TOPK, 1, tb), F32), pltpu.VMEM((PEER_HEADS, PEER_TOPK, 1, tb), F32),
        ],
        compiler_params=_cparams(("parallel",)),
        name="peer_prep",
    )(h2, wq_heads, keys_bf)


_TE = 1024
_ROWS = _TE // N_KEYS


def _peer_dense_kernel(h2_ref, u_ref, vt_ref, s2_ref, e2_ref, th_ref, cc_ref, x1_ref, gt2_ref, o_ref, acc, p_s):
    e = pl.program_id(1)

    @pl.when(e == 0)
    def _():
        acc[...] = jnp.zeros_like(acc)

    base = pl.multiple_of(e * _ROWS, _ROWS)
    th8 = [th_ref[h, pl.ds(base, _ROWS), :] for h in range(PEER_HEADS)]
    cc8 = [cc_ref[h, pl.ds(base, _ROWS), :] for h in range(PEER_HEADS)]
    ht = _dot_nt(u_ref[...], h2_ref[...])
    for r in range(_ROWS):
        rs = slice(r * N_KEYS, (r + 1) * N_KEYS)
        g = None
        for h in range(PEER_HEADS):
            w = jnp.where(s2_ref[h] >= th8[h][r:r + 1], e2_ref[h] * cc8[h][r:r + 1], 0.0)
            g = w if g is None else g + w
        x = ht[rs]
        gelu = 0.5 * x * (1.0 + lax.erf(x * (2.0 ** -0.5)))
        p_s[rs, :] = (g * gelu).astype(BF16)
    acc[...] += _dot(vt_ref[...], p_s[...])

    @pl.when(e == pl.num_programs(1) - 1)
    def _():
        o_ref[...] = x1_ref[...] + gt2_ref[0] * acc[...].T


def _peer_dense(h2, u_bf, vt_bf, s2, e2, th, cc, x1, gt2, tb, rows_per_mod):
    n = h2.shape[0]
    ne = u_bf.shape[0] // _TE
    r = gt2.shape[1]
    tok = pl.BlockSpec((PEER_HEADS, N_KEYS, tb), lambda i, e: (0, 0, i))
    rowblk = pl.BlockSpec((tb, D_MODEL), lambda i, e: (i, 0))
    return pl.pallas_call(
        _peer_dense_kernel,
        grid=(n // tb, ne),
        in_specs=[
            rowblk,
            pl.BlockSpec((_TE, D_MODEL), lambda i, e: (e, 0)),
            pl.BlockSpec((D_MODEL, _TE), lambda i, e: (0, e)),
            tok, tok, tok, tok,
            rowblk,
            pl.BlockSpec((1, r, D_MODEL), lambda i, e: (i // rows_per_mod, 0, 0)),
        ],
        out_specs=rowblk,
        out_shape=jax.ShapeDtypeStruct((n, D_MODEL), F32),
        scratch_shapes=[pltpu.VMEM((D_MODEL, tb), F32), pltpu.VMEM((_TE, tb), BF16)],
        compiler_params=_cparams(("parallel", "arbitrary")),
        name="peer_dense",
    )(h2, u_bf, vt_bf, s2, e2, th, cc, x1, gt2)


def _state_to_pairs(s):
    b = s.shape[0]
    return s.reshape(b, _N_PAIR, 2, HEAD_SIZE_B, HEAD_SIZE_B).transpose(0, 1, 3, 2, 4).reshape(b, _N_PAIR, HEAD_SIZE_B, LANES)


def _pairs_to_state(s):
    b = s.shape[0]
    return s.reshape(b, _N_PAIR, HEAD_SIZE_B, 2, HEAD_SIZE_B).transpose(0, 1, 3, 2, 4).reshape(b, N_HEADS_B, HEAD_SIZE_B, HEAD_SIZE_B)


def _stream(x, mods, w, prm, past, cfg):
    batch, t, _ = x.shape
    n = batch * t
    tm, rpm, tc, tb, rpm_tb = cfg
    sh1, sc1, gt1, sh2, sc2, gt2 = mods
    x2d = x.reshape(n, D_MODEL)
    q, k, v, pr, gates = _in_proj(x2d, sc1, sh1, prm['norm_mix_g'], w['w_in'], prm['q_norm_g'], prm['k_norm_g'], tm, rpm)
    if past is None:
        o_a = _attn_prompt(q, k, v, w['slopes'], prm['lambda_qk'], prm['subln_g'], batch, t, w['lam_init'])
        prev = jnp.zeros((batch, 1, C_SHIFT), F32)
        s0 = jnp.zeros((batch, _N_PAIR, HEAD_SIZE_B, LANES), F32)
    else:
        k_past, v_past, s_past, prev_row = past
        p = k_past.shape[1]
        o_a = _attn_sample(q, k, v, k_past.reshape(batch, p, W_A), v_past.reshape(batch, p, W_A), w['slopes'],
                           prm['lambda_qk'], prm['subln_g'], batch, t, w['lam_init'])
        prev = prev_row.reshape(batch, 1, C_SHIFT)
        s0 = _state_to_pairs(s_past)
    o_b, s_pairs = _rwkv(pr, prev, s0, prm, batch, t, tc)
    x1, h2 = _merge(o_a, o_b, gates, x2d, gt1, sc2, sh2, prm['norm_ffn_g'], w['w_out'], tm if tm <= 512 else 512,
                    rpm if tm <= 512 else rpm * (tm // 512))
    s2, e2, th, cc = _peer_prep(h2, w['wq'], w['keys'], tb)
    y = _peer_dense(h2, w['u'], w['vt'], s2, e2, th, cc, x1, gt2, tb, rpm_tb)
    new_k = k.reshape(batch, t, N_HEADS_A, 2 * HEAD_DIM_A)
    new_v = v.reshape(batch, t, N_HEADS_A, 2 * HEAD_DIM_A)
    shift = pr.reshape(batch, t, C_SHIFT)[:, -1]
    return y.reshape(batch, t, D_MODEL), (new_k, new_v, _pairs_to_state(s_pairs), shift)


def kernel(x_prompt, x_sample, c_prompt, c_sample, cache_attn_k, cache_attn_v, state_rwkv_wkv, state_rwkv_shift, norm_mix_g, norm_ffn_g, w_ada, b_ada, w_in, q_norm_g, k_norm_g, lambda_qk, subln_g, shift_mu, decay_w0, decay_lora_up, iclr_a0, iclr_lora_up, gate_lora_up, k_k, k_a, r_k, lnx_g, lnx_b, w_out, peer_w_query, peer_sub_keys, peer_u, peer_v):
    depth = w_in.shape[0]
    bp, tp, _ = x_prompt.shape
    bs, ts, _ = x_sample.shape
    slopes = jnp.asarray([2.0 ** (-8.0 * (h + 1) / N_HEADS_A) for h in range(N_HEADS_A)], F32)
    st_p, st_s = [], []
    for l in range(depth):
        prm = {
            'norm_mix_g': norm_mix_g[l], 'norm_ffn_g': norm_ffn_g[l], 'q_norm_g': q_norm_g[l], 'k_norm_g': k_norm_g[l],
            'lambda_qk': lambda_qk[l], 'subln_g': subln_g[l], 'shift_mu': shift_mu[l], 'decay_w0': decay_w0[l],
            'decay_lora_up': decay_lora_up[l], 'iclr_a0': iclr_a0[l], 'iclr_lora_up': iclr_lora_up[l],
            'gate_lora_up': gate_lora_up[l], 'k_k': k_k[l], 'k_a': k_a[l], 'r_k': r_k[l], 'lnx_g': lnx_g[l], 'lnx_b': lnx_b[l],
        }
        w = {
            'w_in': w_in[l].astype(BF16),
            'w_out': w_out[l].astype(BF16),
            'wq': peer_w_query[l].reshape(D_MODEL, 2 * PEER_HEADS, N_KEYS).transpose(1, 0, 2).astype(BF16),
            'keys': peer_sub_keys[l].reshape(2 * PEER_HEADS, N_KEYS, N_KEYS).astype(BF16),
            'u': peer_u[l].astype(BF16),
            'vt': peer_v[l].T.astype(BF16),
            'slopes': slopes,
            'lam_init': 0.8 - 0.6 * math.exp(-0.3 * l),
        }
        ada = _ada(jnp.concatenate([c_prompt, c_sample], axis=0), w_ada[l], b_ada[l])
        mods = jnp.split(ada, 6, axis=-1)
        mods_p = [m[:bp].reshape(bp, 1, D_MODEL) for m in mods]
        mods_s = [jnp.repeat(m[bp:], ts, axis=0).reshape(1, bs * ts, D_MODEL) for m in mods]
        x_prompt, sp = _stream(x_prompt, mods_p, w, prm, None, (1024, tp // 1024, CHUNK, 512, tp // 512))
        x_sample, ss = _stream(x_sample, mods_s, w, prm,
                               (cache_attn_k[l], cache_attn_v[l], state_rwkv_wkv[l], state_rwkv_shift[l]),
                               (bs * ts, 1, ts, bs * ts, 1))
        st_p.append(sp)
        st_s.append(ss)
    stack = lambda sts, i: jnp.stack([s[i] for s in sts])
    return (x_prompt, x_sample, stack(st_p, 0), stack(st_p, 1), stack(st_p, 2), stack(st_p, 3),
            stack(st_s, 0), stack(st_s, 1), stack(st_s, 2), stack(st_s, 3))
```

```python
import functools
import math

import jax
import jax.numpy as jnp
from jax import lax
from jax.experimental import pallas as pl
from jax.experimental.pallas import tpu as pltpu

F32 = jnp.float32
BF16 = jnp.bfloat16

D_MODEL = 1024
N_HEADS_A = 8
HEAD_DIM_A = 64
W_A = 1024
N_HEADS_B = 16
HEAD_SIZE_B = 64
W_B = 1024
C_SHIFT = 3328
D_IN = 8448
CHUNK = 64
GN_EPS = 64e-5
N_KEYS = 128
PEER_HEADS = 8
PEER_TOPK = 16
NEG = -1e30
LANES = 128
VMEM_LIMIT = 56 * 1024 * 1024

_NT = (((1,), (1,)), ((), ()))


def _cparams(sem):
    return pltpu.CompilerParams(dimension_semantics=sem, vmem_limit_bytes=VMEM_LIMIT)


def _dot(a, b):
    return jnp.dot(a, b, preferred_element_type=F32)


def _dot_nt(a, b):
    return lax.dot_general(a, b, _NT, preferred_element_type=F32)


def _ada_kernel(c_ref, w_ref, b_ref, o_ref):
    c = c_ref[...]
    s = c * jax.nn.sigmoid(c)
    o_ref[...] = _dot(s.astype(BF16), w_ref[...].astype(BF16)) + b_ref[...]


def _ada(c, w_ada, b_ada):
    n = c.shape[0]
    tn = 1024
    return pl.pallas_call(
        _ada_kernel,
        grid=(6 * D_MODEL // tn,),
        in_specs=[
            pl.BlockSpec((n, D_MODEL), lambda j: (0, 0)),
            pl.BlockSpec((D_MODEL, tn), lambda j: (0, j)),
            pl.BlockSpec((1, tn), lambda j: (0, j)),
        ],
        out_specs=pl.BlockSpec((n, tn), lambda j: (0, j)),
        out_shape=jax.ShapeDtypeStruct((n, 6 * D_MODEL), F32),
        compiler_params=_cparams(("arbitrary",)),
        name="ada",
    )(c, w_ada, b_ada.reshape(1, -1))


_IN_TN = 256
_Q_T, _K_T, _V_T, _PR_T, _G_T = 0, 4, 8, 12, 25
_N_T = D_IN // _IN_TN


def _in_kernel(x_ref, sc_ref, sh_ref, g_ref, w_ref, qg_ref, kg_ref, bd_ref,
               q_ref, k_ref, v_ref, pr_ref, gt_ref, h_scr):
    j = pl.program_id(1)

    @pl.when(j == 0)
    def _():
        x = x_ref[...]
        ms = jnp.mean(x * x, axis=-1, keepdims=True)
        y = x * lax.rsqrt(ms + 1e-6) * g_ref[...]
        h_scr[...] = (y * (1.0 + sc_ref[0]) + sh_ref[0]).astype(BF16)

    acc = _dot(h_scr[...], w_ref[...])

    def head_norm(a, g):
        ms = _dot((a * a).astype(BF16), bd_ref[...])
        return a * lax.rsqrt(ms + 1e-6) * g

    @pl.when(j < _K_T)
    def _():
        q_ref[...] = head_norm(acc, qg_ref[...]).astype(BF16)

    @pl.when((j >= _K_T) & (j < _V_T))
    def _():
        k_ref[...] = head_norm(acc, kg_ref[...])

    @pl.when((j >= _V_T) & (j < _PR_T))
    def _():
        v_ref[...] = acc

    @pl.when((j >= _PR_T) & (j < _G_T))
    def _():
        pr_ref[...] = acc

    @pl.when(j >= _G_T)
    def _():
        gt_ref[...] = jax.nn.sigmoid(acc).astype(BF16)


def _in_proj(x2d, sc, sh, norm_g, w_in_bf, q_g, k_g, tm, rows_per_mod):
    n = x2d.shape[0]
    tn = _IN_TN
    r = sc.shape[1]
    bd = jnp.kron(jnp.eye(tn // HEAD_DIM_A, dtype=F32), jnp.full((HEAD_DIM_A, HEAD_DIM_A), 1.0 / HEAD_DIM_A, F32)).astype(BF16)
    qg = jnp.tile(q_g.reshape(1, -1), (1, tn // HEAD_DIM_A))
    kg = jnp.tile(k_g.reshape(1, -1), (1, tn // HEAD_DIM_A))

    def seg(first, count):
        return lambda i, j: (i, jnp.clip(j - first, 0, count - 1))

    mod_spec = pl.BlockSpec((1, r, D_MODEL), lambda i, j: (i // rows_per_mod, 0, 0))
    return pl.pallas_call(
        _in_kernel,
        grid=(n // tm, _N_T),
        in_specs=[
            pl.BlockSpec((tm, D_MODEL), lambda i, j: (i, 0)),
            mod_spec, mod_spec,
            pl.BlockSpec((1, D_MODEL), lambda i, j: (0, 0)),
            pl.BlockSpec((D_MODEL, tn), lambda i, j: (0, j)),
            pl.BlockSpec((1, tn), lambda i, j: (0, 0)),
            pl.BlockSpec((1, tn), lambda i, j: (0, 0)),
            pl.BlockSpec((tn, tn), lambda i, j: (0, 0)),
        ],
        out_specs=[
            pl.BlockSpec((tm, tn), seg(_Q_T, 4)),
            pl.BlockSpec((tm, tn), seg(_K_T, 4)),
            pl.BlockSpec((tm, tn), seg(_V_T, 4)),
            pl.BlockSpec((tm, tn), seg(_PR_T, 13)),
            pl.BlockSpec((tm, tn), seg(_G_T, 8)),
        ],
        out_shape=[
            jax.ShapeDtypeStruct((n, W_A), BF16),
            jax.ShapeDtypeStruct((n, W_A), F32),
            jax.ShapeDtypeStruct((n, W_A), F32),
            jax.ShapeDtypeStruct((n, C_SHIFT), F32),
            jax.ShapeDtypeStruct((n, 2 * D_MODEL), BF16),
        ],
        scratch_shapes=[pltpu.VMEM((tm, D_MODEL), BF16)],
        compiler_params=_cparams(("parallel", "arbitrary")),
        name="in_proj",
    )(x2d, sc, sh, norm_g.reshape(1, -1), w_in_bf, qg, kg, bd)


_TQ = 512


def _lam(lq_ref, lam_init):
    lq = lq_ref[...]
    a = jnp.sum(lq[0:1] * lq[1:2], axis=-1, keepdims=True)
    b = jnp.sum(lq[2:3] * lq[3:4], axis=-1, keepdims=True)
    return jnp.exp(a) - jnp.exp(b) + lam_init


def _subln(o, sg_ref, lam_init):
    ms = jnp.mean(o * o, axis=-1, keepdims=True)
    return o * lax.rsqrt(ms + 1e-6) * sg_ref[...] * (1.0 - lam_init)


def _split_q(q):
    lane = lax.broadcasted_iota(jnp.int32, q.shape, 1)
    qs = q * jnp.asarray(HEAD_DIM_A ** -0.5, BF16)
    zero = jnp.zeros_like(qs)
    return jnp.where(lane < HEAD_DIM_A, qs, zero), jnp.where(lane >= HEAD_DIM_A, qs, zero)


def _attn_prompt_kernel(slopes_ref, q_ref, k_ref, v_ref, lq_ref, sg_ref, o_ref, kb, vb, *, lam_init):
    h = pl.program_id(1)
    i = pl.program_id(2)
    tq = _TQ

    @pl.when(i == 0)
    def _():
        kb[...] = k_ref[...].astype(BF16)
        vb[:, :LANES] = v_ref[...].astype(BF16)
        vb[:, LANES:] = jnp.ones((vb.shape[0], LANES), BF16)

    slope = slopes_ref[h]
    qc = _split_q(q_ref[...])
    iq = lax.broadcasted_iota(jnp.int32, (tq, tq), 0)
    ik = lax.broadcasted_iota(jnp.int32, (tq, tq), 1)
    rel = (ik - iq).astype(F32)
    bias_off = slope * rel
    visible = (ik // CHUNK) <= (iq // CHUNK)
    bias_diag = jnp.where(visible, -slope * jnp.abs(rel), NEG)

    def block(j, bias, cst, carry):
        kj = kb[pl.ds(pl.multiple_of(j * tq, tq), tq), :]
        vj = vb[pl.ds(pl.multiple_of(j * tq, tq), tq), :]
        out = []
        for c in range(2):
            m, acc = carry[c]
            s = _dot_nt(qc[c], kj) + bias
            m_new = jnp.maximum(m, jnp.max(s, axis=-1, keepdims=True) + cst)
            p = jnp.exp(s - (m_new - cst))
            acc = acc * jnp.exp(m - m_new) + _dot(p.astype(BF16), vj)
            out.append((m_new, acc))
        return tuple(out)

    init = tuple((jnp.full((tq, 1), NEG, F32), jnp.zeros((tq, 2 * LANES), F32)) for _ in range(2))

    def off_body(j, carry):
        cst = -slope * ((i - j) * tq).astype(F32)
        return block(j, bias_off, cst, carry)

    carry = lax.fori_loop(0, i, off_body, init)
    carry = block(i, bias_diag, jnp.zeros((), F32), carry)

    lam = _lam(lq_ref, lam_init)
    o = [acc[:, :LANES] / acc[:, LANES:LANES + 1] for (_, acc) in carry]
    o_ref[...] = _subln(o[0] - lam * o[1], sg_ref, lam_init)


def _attn_prompt(q, k, v, slopes, lambda_qk, subln_g, batch, seq, lam_init):
    n = batch * seq
    nq = seq // _TQ
    kern = functools.partial(_attn_prompt_kernel, lam_init=lam_init)
    return pl.pallas_call(
        kern,
        grid=(batch, N_HEADS_A, nq),
        in_specs=[
            pl.BlockSpec(memory_space=pltpu.SMEM),
            pl.BlockSpec((_TQ, LANES), lambda b, h, i: (b * nq + i, h)),
            pl.BlockSpec((seq, LANES), lambda b, h, i: (b, h)),
            pl.BlockSpec((seq, LANES), lambda b, h, i: (b, h)),
            pl.BlockSpec((4, HEAD_DIM_A), lambda b, h, i: (0, 0)),
            pl.BlockSpec((1, LANES), lambda b, h, i: (0, 0)),
        ],
        out_specs=pl.BlockSpec((_TQ, LANES), lambda b, h, i: (b * nq + i, h)),
        out_shape=jax.ShapeDtypeStruct((n, W_A), F32),
        scratch_shapes=[pltpu.VMEM((seq, LANES), BF16), pltpu.VMEM((seq, 2 * LANES), BF16)],
        compiler_params=_cparams(("parallel", "parallel", "arbitrary")),
        name="attn_prompt",
    )(slopes, q, k, v, lambda_qk, subln_g.reshape(1, -1))


def _attn_sample_kernel(slopes_ref, q_ref, kp_ref, vp_ref, kn_ref, vn_ref, lq_ref, sg_ref, o_ref, *, lam_init, past):
    t = q_ref.shape[0]
    lam = _lam(lq_ref, lam_init)

    def bias(nk, k0, slope):
        qpos = past + lax.broadcasted_iota(jnp.int32, (t, nk), 0)
        kpos = k0 + lax.broadcasted_iota(jnp.int32, (t, nk), 1)
        vis = (kpos // CHUNK) <= (qpos // CHUNK)
        return jnp.where(vis, -slope * jnp.abs(qpos - kpos).astype(F32), NEG)

    for h in range(N_HEADS_A):
        sl = slice(h * LANES, (h + 1) * LANES)
        slope = slopes_ref[h]
        qc = _split_q(q_ref[:, sl])
        kp = kp_ref[0, :, h, :].astype(BF16)
        vp = vp_ref[0, :, h, :].astype(BF16)
        kn = kn_ref[:, sl].astype(BF16)
        vn = vn_ref[:, sl].astype(BF16)
        bp = bias(past, 0, slope)
        bn = bias(t, past, slope)
        o = []
        for c in range(2):
            sp = _dot_nt(qc[c], kp) + bp
            sn = _dot_nt(qc[c], kn) + bn
            m = jnp.maximum(jnp.max(sp, axis=-1, keepdims=True), jnp.max(sn, axis=-1, keepdims=True))
            pp = jnp.exp(sp - m)
            pn = jnp.exp(sn - m)
            l = jnp.sum(pp, axis=-1, keepdims=True) + jnp.sum(pn, axis=-1, keepdims=True)
            o.append((_dot(pp.astype(BF16), vp) + _dot(pn.astype(BF16), vn)) / l)
        o_ref[:, sl] = _subln(o[0] - lam * o[1], sg_ref, lam_init)


def _attn_sample(q, k_new, v_new, k_past, v_past, slopes, lambda_qk, subln_g, batch, t, lam_init):
    past = k_past.shape[1]
    kern = functools.partial(_attn_sample_kernel, lam_init=lam_init, past=past)
    tok = pl.BlockSpec((t, W_A), lambda b: (b, 0))
    cache = pl.BlockSpec((1, past, N_HEADS_A, LANES), lambda b: (b, 0, 0, 0))
    return pl.pallas_call(
        kern,
        grid=(batch,),
        in_specs=[
            pl.BlockSpec(memory_space=pltpu.SMEM),
            tok, cache, cache, tok, tok,
            pl.BlockSpec((4, HEAD_DIM_A), lambda b: (0, 0)),
            pl.BlockSpec((1, LANES), lambda b: (0, 0)),
        ],
        out_specs=tok,
        out_shape=jax.ShapeDtypeStruct((batch * t, W_A), F32),
        compiler_params=_cparams(("parallel",)),
        name="attn_sample",
    )(slopes, q, k_past, v_past, k_new, v_new, lambda_qk, subln_g.reshape(1, -1))


_N_PAIR = N_HEADS_B // 2
_XW_COL = 3 * W_B
_XG_COL = 3 * W_B + LANES
_GRP = 8


def _rwkv_kernel(pr_ref, prev_ref, s0_ref, mu_ref, w0_ref, wd_ref, a0_ref, wa_ref, wg_ref, kk_ref, ka_ref, rk_ref,
                 lg_ref, lb_ref, bd_ref, o4_ref, eye_ref, sel_ref,
                 ob_ref, sout_ref,
                 s_s, prev_s, r_s, w_s, k_s, kk_s, b_s, g_s, bon_s, vt_s, yb_s):
    c = pl.program_id(1)
    nb, tc, _ = pr_ref.shape
    n_pair = nb * _N_PAIR

    @pl.when(c == 0)
    def _():
        prev_s[...] = prev_ref[...]
        s_s[...] = s0_ref[...].reshape(s_s.shape)

    bd = bd_ref[...]
    pad = HEAD_SIZE_B - tc
    for bb in range(nb):
        pr = pr_ref[bb]
        row = lax.broadcasted_iota(jnp.int32, pr.shape, 0)
        shifted = jnp.where(row == 0, prev_s[bb], pltpu.roll(pr, 1, 0))
        prev_s[bb] = pr[tc - 1:tc, :]
        xs = pr + (shifted - pr) * mu_ref[...]
        r = xs[:, 0:W_B]
        k = xs[:, W_B:2 * W_B]
        v = xs[:, 2 * W_B:3 * W_B]
        xwa = xs[:, _XW_COL:_XW_COL + LANES]
        xg = xs[:, _XG_COL:_XG_COL + LANES]
        z = w0_ref[...] + _dot(jnp.tanh(xwa).astype(BF16), wd_ref[...])
        wlog = -(jnp.maximum(-z, 0.0) + jnp.log(1.0 + jnp.exp(-jnp.abs(z)))) - 0.5
        decay = jnp.exp(-jnp.exp(wlog))
        a = jax.nn.sigmoid(a0_ref[...] + _dot(xwa.astype(BF16), wa_ref[...]))
        g_s[bb] = _dot(jax.nn.sigmoid(xg).astype(BF16), wg_ref[...])
        kkraw = k * kk_ref[...]
        k_mod = k * (1.0 + (a - 1.0) * ka_ref[...])
        rkr = r * k_mod * rk_ref[...]
        for p in range(_N_PAIR):
            sl = slice(p * LANES, (p + 1) * LANES)
            kq = kkraw[:, sl]
            ss = _dot((kq * kq).astype(BF16), bd)
            kkn = kq / jnp.maximum(jnp.sqrt(ss), 1e-12)
            kk_s[bb, :, sl] = kkn
            b_s[bb, :, sl] = kkn * a[:, sl]
            bon_s[bb, :, sl] = _dot(rkr[:, sl].astype(BF16), bd) * v[:, sl]
            vp = v[:, sl]
            lane = lax.broadcasted_iota(jnp.int32, vp.shape, 1)
            parts = []
            for hh in range(2):
                parts.append(jnp.where((lane >= HEAD_SIZE_B) == (hh == 1), vp, 0.0))
                if pad:
                    parts.append(jnp.zeros((pad, LANES), F32))
            vt_s[bb * _N_PAIR + p] = _dot_nt(eye_ref[...], jnp.concatenate(parts, axis=0).astype(BF16))
        r_s[bb] = r
        w_s[bb] = decay
        k_s[bb] = k_mod
    yb_s[...] = jnp.zeros_like(yb_s)

    ones4 = o4_ref[...]
    lane_t = lax.broadcasted_iota(jnp.int32, (HEAD_SIZE_B, LANES), 1) % HEAD_SIZE_B
    sls = [slice(p * LANES, (p + 1) * LANES) for p in range(_N_PAIR)]
    n_half = _N_PAIR // 2

    def stack(mats):
        return jnp.concatenate([jnp.concatenate([mats[2 * q], mats[2 * q + 1]], axis=1) for q in range(len(mats) // 2)], axis=0)

    def unstack(m, j):
        q, o = divmod(j, 2)
        return m[q * HEAD_SIZE_B:(q + 1) * HEAD_SIZE_B, o * LANES:(o + 1) * LANES]

    def group(gi, carry):
        base = pl.multiple_of(gi * _GRP, _GRP)
        rows = lambda ref: [ref[bb, pl.ds(base, _GRP), :] for bb in range(nb)]
        kk8, w8, b8, k8, r8 = rows(kk_s), rows(w_s), rows(b_s), rows(k_s), rows(r_s)
        row = lambda tiles, p, i: tiles[p // _N_PAIR][i:i + 1, sls[p % _N_PAIR]]
        s = [s_s[p] for p in range(n_pair)]
        for i in range(_GRP):
            tmask = lane_t == (base + i)
            vcol = []
            for bb in range(nb):
                vc = _dot(stack([jnp.where(tmask, vt_s[bb * _N_PAIR + p], 0.0).astype(BF16) for p in range(_N_PAIR)]), ones4)
                vcol += [unstack(vc, p) for p in range(_N_PAIR)]
            for half in range(2 * nb):
                ps = list(range(half * n_half, (half + 1) * n_half))
                sa = _dot(stack([(s[p] * row(kk8, p, i)).astype(BF16) for p in ps]), ones4)
                for j, p in enumerate(ps):
                    s[p] = s[p] * row(w8, p, i) - unstack(sa, j) * row(b8, p, i) + vcol[p] * row(k8, p, i)
            for bb in range(nb):
                ps = list(range(bb * _N_PAIR, (bb + 1) * _N_PAIR))
                yb = _dot(stack([(s[p] * row(r8, p, i)).astype(BF16) for p in ps]), ones4)
                for j, p in enumerate(ps):
                    yb_s[p] = jnp.where(tmask, unstack(yb, j), yb_s[p])
        for p in range(n_pair):
            s_s[p] = s[p]
        return carry

    lax.fori_loop(0, tc // _GRP, group, 0)

    for bb in range(nb):
        for h in range(N_HEADS_B):
            sl = slice(h * HEAD_SIZE_B, (h + 1) * HEAD_SIZE_B)
            ybp = yb_s[bb * _N_PAIR + h // 2]
            hi = ybp.astype(BF16)
            lo = (ybp - hi.astype(F32)).astype(BF16)
            sel = sel_ref[h % 2][:tc]
            y = _dot_nt(sel, hi) + _dot_nt(sel, lo)
            m = jnp.mean(y, axis=-1, keepdims=True)
            var = jnp.mean(jnp.square(y - m), axis=-1, keepdims=True)
            yn = (y - m) * lax.rsqrt(var + GN_EPS) * lg_ref[h] + lb_ref[h]
            ob_ref[bb, :, sl] = (yn + bon_s[bb, :, sl]) * g_s[bb, :, sl]

    @pl.when(c == pl.num_programs(1) - 1)
    def _():
        sout_ref[...] = s_s[...].reshape(sout_ref.shape)


_RWKV_NB = 2


def _rwkv(pr, prev_row, s0_pairs, prm, batch, seq, tc):
    n = batch * seq
    nt = seq // tc
    nb = _RWKV_NB
    row = lambda a: a.reshape(1, -1).astype(F32)
    zeros = jnp.zeros((HEAD_SIZE_B, W_B), F32)
    wd = jnp.concatenate([prm['decay_lora_up'], zeros], axis=0).astype(BF16)
    wa = jnp.concatenate([zeros, prm['iclr_lora_up']], axis=0).astype(BF16)
    wg = prm['gate_lora_up'].astype(BF16)
    ones_h = jnp.ones((HEAD_SIZE_B, HEAD_SIZE_B), F32)
    bd = jnp.kron(jnp.eye(2, dtype=F32), ones_h).astype(BF16)
    ones4 = jnp.kron(jnp.eye(4, dtype=F32), ones_h).astype(BF16)
    eye2 = jnp.concatenate([jnp.eye(HEAD_SIZE_B, dtype=F32)] * 2, axis=1).astype(BF16)
    sel = jnp.eye(LANES, dtype=F32).reshape(2, HEAD_SIZE_B, LANES).astype(BF16)
    lg = prm['lnx_g'].reshape(N_HEADS_B, 1, HEAD_SIZE_B)
    lb = prm['lnx_b'].reshape(N_HEADS_B, 1, HEAD_SIZE_B)
    full = lambda shape: pl.BlockSpec(shape, lambda b, c: (0,) * len(shape))
    vec = full((1, W_B))
    state_spec = pl.BlockSpec((nb, _N_PAIR, HEAD_SIZE_B, LANES), lambda b, c: (b, 0, 0, 0))
    o_b, s_new = pl.pallas_call(
        _rwkv_kernel,
        grid=(batch // nb, nt),
        in_specs=[
            pl.BlockSpec((nb, tc, C_SHIFT), lambda b, c: (b, c, 0)),
            pl.BlockSpec((nb, 1, C_SHIFT), lambda b, c: (b, 0, 0)),
            state_spec,
            full((1, C_SHIFT)), vec, full((LANES, W_B)), vec, full((LANES, W_B)), full((LANES, W_B)),
            vec, vec, vec,
            full((N_HEADS_B, 1, HEAD_SIZE_B)), full((N_HEADS_B, 1, HEAD_SIZE_B)),
            full((LANES, LANES)), full((2 * LANES, 2 * LANES)), full((HEAD_SIZE_B, LANES)), full((2, HEAD_SIZE_B, LANES)),
        ],
        out_specs=[
            pl.BlockSpec((nb, tc, W_B), lambda b, c: (b, c, 0)),
            state_spec,
        ],
        out_shape=[
            jax.ShapeDtypeStruct((batch, seq, W_B), F32),
            jax.ShapeDtypeStruct((batch, _N_PAIR, HEAD_SIZE_B, LANES), F32),
        ],
        scratch_shapes=[
            pltpu.VMEM((nb * _N_PAIR, HEAD_SIZE_B, LANES), F32),
            pltpu.VMEM((nb, 1, C_SHIFT), F32),
        ] + [pltpu.VMEM((nb, tc, W_B), F32)] * 7 + [
            pltpu.VMEM((nb * _N_PAIR, HEAD_SIZE_B, LANES), F32),
            pltpu.VMEM((nb * _N_PAIR, HEAD_SIZE_B, LANES), F32),
        ],
        compiler_params=_cparams(("parallel", "arbitrary")),
        name="rwkv",
    )(pr.reshape(batch, seq, C_SHIFT), prev_row, s0_pairs, row(prm['shift_mu']), row(prm['decay_w0']), wd, row(prm['iclr_a0']),
      wa, wg, row(prm['k_k']), row(prm['k_a']), row(prm['r_k']), lg, lb, bd, ones4, eye2, sel)
    return o_b.reshape(n, W_B), s_new


def _merge_kernel(oa_ref, ob_ref, g_ref, x_ref, gt1_ref, sc2_ref, sh2_ref, ng_ref, w_ref, x1_ref, h2_ref):
    g = g_ref[...].astype(F32)
    mixed = (g[:, :D_MODEL] * oa_ref[...] + g[:, D_MODEL:] * ob_ref[...]).astype(BF16)
    x1 = x_ref[...] + gt1_ref[0] * _dot(mixed, w_ref[...])
    x1_ref[...] = x1
    ms = jnp.mean(x1 * x1, axis=-1, keepdims=True)
    y = x1 * lax.rsqrt(ms + 1e-6) * ng_ref[...]
    h2_ref[...] = (y * (1.0 + sc2_ref[0]) + sh2_ref[0]).astype(BF16)


def _merge(o_a, o_b, gates, x2d, gt1, sc2, sh2, norm_g, w_out_bf, tm, rows_per_mod):
    n = x2d.shape[0]
    r = gt1.shape[1]
    rowblk = pl.BlockSpec((tm, D_MODEL), lambda i: (i, 0))
    mod_spec = pl.BlockSpec((1, r, D_MODEL), lambda i: (i // rows_per_mod, 0, 0))
    return pl.pallas_call(
        _merge_kernel,
        grid=(n // tm,),
        in_specs=[
            rowblk, rowblk,
            pl.BlockSpec((tm, 2 * D_MODEL), lambda i: (i, 0)),
            rowblk, mod_spec, mod_spec, mod_spec,
            pl.BlockSpec((1, D_MODEL), lambda i: (0, 0)),
            pl.BlockSpec((D_MODEL, D_MODEL), lambda i: (0, 0)),
        ],
        out_specs=[rowblk, rowblk],
        out_shape=[jax.ShapeDtypeStruct((n, D_MODEL), F32), jax.ShapeDtypeStruct((n, D_MODEL), BF16)],
        compiler_params=_cparams(("parallel",)),
        name="merge",
    )(o_a, o_b, gates, x2d, gt1, sc2, sh2, norm_g.reshape(1, -1), w_out_bf)


_NTOP = PEER_TOPK + 1
_CAND = [(a, b) for a in range(_NTOP) for b in range(_NTOP) if (a + 1) * (b + 1) <= _NTOP]


def _peer_prep_kernel(h2_ref, wq_ref, keys_ref, s2_ref, e2_ref, th_ref, cc_ref, s1_s, s2_s, v1_s, v2_s):
    h2 = h2_ref[...]
    ninf = -jnp.inf

    def head(h, carry):
        for c, (s_s, v_s) in enumerate(((s1_s, v1_s), (s2_s, v2_s))):
            qh = _dot(h2, wq_ref[2 * h + c]).astype(BF16)
            s = _dot_nt(keys_ref[2 * h + c], qh)
            s_s[h] = s
            cur = s
            for a in range(_NTOP):
                m = jnp.max(cur, axis=0, keepdims=True)
                v_s[h, a] = m
                cur = jnp.where(cur == m, ninf, cur)
        return carry

    lax.fori_loop(0, PEER_HEADS, head, 0)

    v1 = [jnp.concatenate([v1_s[h, a] for h in range(PEER_HEADS)], axis=0) for a in range(_NTOP)]
    v2 = [jnp.concatenate([v2_s[h, a] for h in range(PEER_HEADS)], axis=0) for a in range(_NTOP)]
    cand = [v1[a] + v2[b] for (a, b) in _CAND]
    tops = []
    for _ in range(_NTOP):
        m = cand[0]
        for cv in cand[1:]:
            m = jnp.maximum(m, cv)
        tops.append(m)
        rem = jnp.ones_like(m)
        nxt = []
        for cv in cand:
            hit = jnp.where(cv == m, rem, 0.0)
            nxt.append(jnp.where(hit > 0.0, ninf, cv))
            rem = rem - hit
        cand = nxt
    tau = jnp.where(tops[PEER_TOPK] == ninf, tops[PEER_TOPK - 1], 0.5 * (tops[PEER_TOPK - 1] + tops[PEER_TOPK]))
    z = jnp.exp(tops[0] - tops[0])
    for a in range(1, PEER_TOPK):
        z = z + jnp.exp(tops[a] - tops[0])
    rz = 1.0 / z
    for h in range(PEER_HEADS):
        s1 = s1_s[h]
        s2 = s2_s[h]
        s2_ref[h] = s2
        e2_ref[h] = jnp.exp(s2 - v2[0][h:h + 1]).astype(BF16)
        th_ref[h] = tau[h:h + 1] - s1
        cc_ref[h] = (jnp.exp(s1 - v1[0][h:h + 1]) * rz[h:h + 1]).astype(BF16)


def _peer_prep(h2, wq_heads, keys_bf, tb):
    n = h2.shape[0]
    out = jax.ShapeDtypeStruct((PEER_HEADS, N_KEYS, n), F32)
    out_bf = jax.ShapeDtypeStruct((PEER_HEADS, N_KEYS, n), BF16)
    ospec = pl.BlockSpec((PEER_HEADS, N_KEYS, tb), lambda i: (0, 0, i))
    return pl.pallas_call(
        _peer_prep_kernel,
        grid=(n // tb,),
        in_specs=[
            pl.BlockSpec((tb, D_MODEL), lambda i: (i, 0)),
            pl.BlockSpec((2 * PEER_HEADS, D_MODEL, N_KEYS), lambda i: (0, 0, 0)),
            pl.BlockSpec((2 * PEER_HEADS, N_KEYS, N_KEYS), lambda i: (0, 0, 0)),
        ],
        out_specs=[ospec] * 4,
        out_shape=[out, out_bf, out, out_bf],
        scratch_shapes=[
            pltpu.VMEM((PEER_HEADS, N_KEYS, tb), F32), pltpu.VMEM((PEER_HEADS, N_KEYS, tb), F32),
            pltpu.VMEM((PEER_HEADS, _NTOP, 1, tb), F32), pltpu.VMEM((PEER_HEADS, _NTOP, 1, tb), F32),
        ],
        compiler_params=_cparams(("parallel",)),
        name="peer_prep",
    )(h2, wq_heads, keys_bf)


_TE = 1024
_ROWS = _TE // N_KEYS


def _peer_dense_kernel(h2_ref, u_ref, vt_ref, s2_ref, e2_ref, th_ref, cc_ref, x1_ref, gt2_ref, o_ref, acc, p_s):
    e = pl.program_id(1)

    @pl.when(e == 0)
    def _():
        acc[...] = jnp.zeros_like(acc)

    base = pl.multiple_of(e * _ROWS, _ROWS)
    th8 = [th_ref[h, pl.ds(base, _ROWS), :] for h in range(PEER_HEADS)]
    cc8 = [cc_ref[h, pl.ds(base, _ROWS), :] for h in range(PEER_HEADS)]
    ht = _dot_nt(u_ref[...], h2_ref[...])
    for r in range(_ROWS):
        rs = slice(r * N_KEYS, (r + 1) * N_KEYS)
        g = None
        for h in range(PEER_HEADS):
            w = jnp.where(s2_ref[h] >= th8[h][r:r + 1], e2_ref[h] * cc8[h][r:r + 1], jnp.zeros((), BF16))
            g = w if g is None else g + w
        x = ht[rs]
        gelu = 0.5 * x * (1.0 + lax.erf(x * (2.0 ** -0.5)))
        p_s[rs, :] = g * gelu.astype(BF16)
    acc[...] += _dot(vt_ref[...], p_s[...])

    @pl.when(e == pl.num_programs(1) - 1)
    def _():
        o_ref[...] = x1_ref[...] + gt2_ref[0] * acc[...].T


def _peer_dense(h2, u_bf, vt_bf, s2, e2, th, cc, x1, gt2, tb, rows_per_mod):
    n = h2.shape[0]
    ne = u_bf.shape[0] // _TE
    r = gt2.shape[1]
    tok = pl.BlockSpec((PEER_HEADS, N_KEYS, tb), lambda i, e: (0, 0, i))
    rowblk = pl.BlockSpec((tb, D_MODEL), lambda i, e: (i, 0))
    return pl.pallas_call(
        _peer_dense_kernel,
        grid=(n // tb, ne),
        in_specs=[
            rowblk,
            pl.BlockSpec((_TE, D_MODEL), lambda i, e: (e, 0)),
            pl.BlockSpec((D_MODEL, _TE), lambda i, e: (0, e)),
            tok, tok, tok, tok,
            rowblk,
            pl.BlockSpec((1, r, D_MODEL), lambda i, e: (i // rows_per_mod, 0, 0)),
        ],
        out_specs=rowblk,
        out_shape=jax.ShapeDtypeStruct((n, D_MODEL), F32),
        scratch_shapes=[pltpu.VMEM((D_MODEL, tb), F32), pltpu.VMEM((_TE, tb), BF16)],
        compiler_params=_cparams(("parallel", "arbitrary")),
        name="peer_dense",
    )(h2, u_bf, vt_bf, s2, e2, th, cc, x1, gt2)


def _state_to_pairs(s):
    b = s.shape[0]
    return s.reshape(b, _N_PAIR, 2, HEAD_SIZE_B, HEAD_SIZE_B).transpose(0, 1, 3, 2, 4).reshape(b, _N_PAIR, HEAD_SIZE_B, LANES)


def _pairs_to_state(s):
    b = s.shape[0]
    return s.reshape(b, _N_PAIR, HEAD_SIZE_B, 2, HEAD_SIZE_B).transpose(0, 1, 3, 2, 4).reshape(b, N_HEADS_B, HEAD_SIZE_B, HEAD_SIZE_B)


def _stream(x, mods, w, prm, past, cfg):
    batch, t, _ = x.shape
    n = batch * t
    tm, rpm, tc, tb, rpm_tb = cfg
    sh1, sc1, gt1, sh2, sc2, gt2 = mods
    x2d = x.reshape(n, D_MODEL)
    q, k, v, pr, gates = _in_proj(x2d, sc1, sh1, prm['norm_mix_g'], w['w_in'], prm['q_norm_g'], prm['k_norm_g'], tm, rpm)
    if past is None:
        o_a = _attn_prompt(q, k, v, w['slopes'], prm['lambda_qk'], prm['subln_g'], batch, t, w['lam_init'])
        prev = jnp.zeros((batch, 1, C_SHIFT), F32)
        s0 = jnp.zeros((batch, _N_PAIR, HEAD_SIZE_B, LANES), F32)
    else:
        k_past, v_past, s_past, prev_row = past
        o_a = _attn_sample(q, k, v, k_past, v_past, w['slopes'],
                           prm['lambda_qk'], prm['subln_g'], batch, t, w['lam_init'])
        prev = prev_row.reshape(batch, 1, C_SHIFT)
        s0 = _state_to_pairs(s_past)
    o_b, s_pairs = _rwkv(pr, prev, s0, prm, batch, t, tc)
    x1, h2 = _merge(o_a, o_b, gates, x2d, gt1, sc2, sh2, prm['norm_ffn_g'], w['w_out'], tm if tm <= 512 else 512,
                    rpm if tm <= 512 else rpm * (tm // 512))
    s2, e2, th, cc = _peer_prep(h2, w['wq'], w['keys'], tb)
    y = _peer_dense(h2, w['u'], w['vt'], s2, e2, th, cc, x1, gt2, tb, rpm_tb)
    new_k = k.reshape(batch, t, N_HEADS_A, 2 * HEAD_DIM_A)
    new_v = v.reshape(batch, t, N_HEADS_A, 2 * HEAD_DIM_A)
    shift = pr.reshape(batch, t, C_SHIFT)[:, -1]
    return y.reshape(batch, t, D_MODEL), (new_k, new_v, _pairs_to_state(s_pairs), shift)


def kernel(x_prompt, x_sample, c_prompt, c_sample, cache_attn_k, cache_attn_v, state_rwkv_wkv, state_rwkv_shift, norm_mix_g, norm_ffn_g, w_ada, b_ada, w_in, q_norm_g, k_norm_g, lambda_qk, subln_g, shift_mu, decay_w0, decay_lora_up, iclr_a0, iclr_lora_up, gate_lora_up, k_k, k_a, r_k, lnx_g, lnx_b, w_out, peer_w_query, peer_sub_keys, peer_u, peer_v):
    depth = w_in.shape[0]
    bp, tp, _ = x_prompt.shape
    bs, ts, _ = x_sample.shape
    slopes = jnp.asarray([2.0 ** (-8.0 * (h + 1) / N_HEADS_A) for h in range(N_HEADS_A)], F32)
    st_p, st_s = [], []
    for l in range(depth):
        prm = {
            'norm_mix_g': norm_mix_g[l], 'norm_ffn_g': norm_ffn_g[l], 'q_norm_g': q_norm_g[l], 'k_norm_g': k_norm_g[l],
            'lambda_qk': lambda_qk[l], 'subln_g': subln_g[l], 'shift_mu': shift_mu[l], 'decay_w0': decay_w0[l],
            'decay_lora_up': decay_lora_up[l], 'iclr_a0': iclr_a0[l], 'iclr_lora_up': iclr_lora_up[l],
            'gate_lora_up': gate_lora_up[l], 'k_k': k_k[l], 'k_a': k_a[l], 'r_k': r_k[l], 'lnx_g': lnx_g[l], 'lnx_b': lnx_b[l],
        }
        w = {
            'w_in': w_in[l].astype(BF16),
            'w_out': w_out[l].astype(BF16),
            'wq': peer_w_query[l].reshape(D_MODEL, 2 * PEER_HEADS, N_KEYS).transpose(1, 0, 2).astype(BF16),
            'keys': peer_sub_keys[l].reshape(2 * PEER_HEADS, N_KEYS, N_KEYS).astype(BF16),
            'u': peer_u[l].astype(BF16),
            'vt': peer_v[l].T.astype(BF16),
            'slopes': slopes,
            'lam_init': 0.8 - 0.6 * math.exp(-0.3 * l),
        }
        ada = _ada(jnp.concatenate([c_prompt, c_sample], axis=0), w_ada[l], b_ada[l])
        mods = jnp.split(ada, 6, axis=-1)
        mods_p = [m[:bp].reshape(bp, 1, D_MODEL) for m in mods]
        mods_s = [jnp.repeat(m[bp:], ts, axis=0).reshape(1, bs * ts, D_MODEL) for m in mods]
        x_prompt, sp = _stream(x_prompt, mods_p, w, prm, None, (2048, tp // 2048, CHUNK, 512, tp // 512))
        x_sample, ss = _stream(x_sample, mods_s, w, prm,
                               (cache_attn_k[l], cache_attn_v[l], state_rwkv_wkv[l], state_rwkv_shift[l]),
                               (bs * ts, 1, ts, bs * ts, 1))
        st_p.append(sp)
        st_s.append(ss)
    stack = lambda sts, i: jnp.stack([s[i] for s in sts])
    return (x_prompt, x_sample, stack(st_p, 0), stack(st_p, 1), stack(st_p, 2), stack(st_p, 3),
            stack(st_s, 0), stack(st_s, 1), stack(st_s, 2), stack(st_s, 3))
```

```python
import functools
import math

import jax
import jax.numpy as jnp
from jax import lax
from jax.experimental import pallas as pl
from jax.experimental.pallas import tpu as pltpu

F32 = jnp.float32
BF16 = jnp.bfloat16

D_MODEL = 1024
N_HEADS_A = 8
HEAD_DIM_A = 64
W_A = 1024
N_HEADS_B = 16
HEAD_SIZE_B = 64
W_B = 1024
C_SHIFT = 3328
D_IN = 8448
CHUNK = 64
GN_EPS = 64e-5
N_KEYS = 128
PEER_HEADS = 8
PEER_TOPK = 16
NEG = -1e30
LANES = 128
VMEM_LIMIT = 56 * 1024 * 1024

_NT = (((1,), (1,)), ((), ()))


def _cparams(sem):
    return pltpu.CompilerParams(dimension_semantics=sem, vmem_limit_bytes=VMEM_LIMIT)


def _dot(a, b):
    return jnp.dot(a, b, preferred_element_type=F32)


def _dot_nt(a, b):
    return lax.dot_general(a, b, _NT, preferred_element_type=F32)


def _ada_kernel(c_ref, w_ref, b_ref, o_ref):
    c = c_ref[...]
    s = c * jax.nn.sigmoid(c)
    o_ref[...] = _dot(s.astype(BF16), w_ref[...].astype(BF16)) + b_ref[...]


def _ada(c, w_ada, b_ada):
    n = c.shape[0]
    tn = 1024
    return pl.pallas_call(
        _ada_kernel,
        grid=(6 * D_MODEL // tn,),
        in_specs=[
            pl.BlockSpec((n, D_MODEL), lambda j: (0, 0)),
            pl.BlockSpec((D_MODEL, tn), lambda j: (0, j)),
            pl.BlockSpec((1, tn), lambda j: (0, j)),
        ],
        out_specs=pl.BlockSpec((n, tn), lambda j: (0, j)),
        out_shape=jax.ShapeDtypeStruct((n, 6 * D_MODEL), F32),
        compiler_params=_cparams(("arbitrary",)),
        name="ada",
    )(c, w_ada, b_ada.reshape(1, -1))


_IN_TN = 256
_Q_T, _K_T, _V_T, _PR_T, _G_T = 0, 4, 8, 12, 25
_N_T = D_IN // _IN_TN


def _in_kernel(x_ref, sc_ref, sh_ref, g_ref, w_ref, qg_ref, kg_ref, bd_ref,
               q_ref, k_ref, v_ref, pr_ref, gt_ref, h_scr):
    j = pl.program_id(1)

    @pl.when(j == 0)
    def _():
        x = x_ref[...]
        ms = jnp.mean(x * x, axis=-1, keepdims=True)
        y = x * lax.rsqrt(ms + 1e-6) * g_ref[...]
        h_scr[...] = (y * (1.0 + sc_ref[0]) + sh_ref[0]).astype(BF16)

    acc = _dot(h_scr[...], w_ref[...])

    def head_norm(a, g):
        ms = _dot((a * a).astype(BF16), bd_ref[...])
        return a * lax.rsqrt(ms + 1e-6) * g

    @pl.when(j < _K_T)
    def _():
        q_ref[...] = head_norm(acc, qg_ref[...]).astype(BF16)

    @pl.when((j >= _K_T) & (j < _V_T))
    def _():
        k_ref[...] = head_norm(acc, kg_ref[...])

    @pl.when((j >= _V_T) & (j < _PR_T))
    def _():
        v_ref[...] = acc

    @pl.when((j >= _PR_T) & (j < _G_T))
    def _():
        pr_ref[...] = acc

    @pl.when(j >= _G_T)
    def _():
        gt_ref[...] = jax.nn.sigmoid(acc).astype(BF16)


def _in_proj(x2d, sc, sh, norm_g, w_in_bf, q_g, k_g, tm, rows_per_mod):
    n = x2d.shape[0]
    tn = _IN_TN
    r = sc.shape[1]
    bd = jnp.kron(jnp.eye(tn // HEAD_DIM_A, dtype=F32), jnp.full((HEAD_DIM_A, HEAD_DIM_A), 1.0 / HEAD_DIM_A, F32)).astype(BF16)
    qg = jnp.tile(q_g.reshape(1, -1), (1, tn // HEAD_DIM_A))
    kg = jnp.tile(k_g.reshape(1, -1), (1, tn // HEAD_DIM_A))

    def seg(first, count):
        return lambda i, j: (i, jnp.clip(j - first, 0, count - 1))

    mod_spec = pl.BlockSpec((1, r, D_MODEL), lambda i, j: (i // rows_per_mod, 0, 0))
    return pl.pallas_call(
        _in_kernel,
        grid=(n // tm, _N_T),
        in_specs=[
            pl.BlockSpec((tm, D_MODEL), lambda i, j: (i, 0)),
            mod_spec, mod_spec,
            pl.BlockSpec((1, D_MODEL), lambda i, j: (0, 0)),
            pl.BlockSpec((D_MODEL, tn), lambda i, j: (0, j)),
            pl.BlockSpec((1, tn), lambda i, j: (0, 0)),
            pl.BlockSpec((1, tn), lambda i, j: (0, 0)),
            pl.BlockSpec((tn, tn), lambda i, j: (0, 0)),
        ],
        out_specs=[
            pl.BlockSpec((tm, tn), seg(_Q_T, 4)),
            pl.BlockSpec((tm, tn), seg(_K_T, 4)),
            pl.BlockSpec((tm, tn), seg(_V_T, 4)),
            pl.BlockSpec((tm, tn), seg(_PR_T, 13)),
            pl.BlockSpec((tm, tn), seg(_G_T, 8)),
        ],
        out_shape=[
            jax.ShapeDtypeStruct((n, W_A), BF16),
            jax.ShapeDtypeStruct((n, W_A), F32),
            jax.ShapeDtypeStruct((n, W_A), F32),
            jax.ShapeDtypeStruct((n, C_SHIFT), F32),
            jax.ShapeDtypeStruct((n, 2 * D_MODEL), BF16),
        ],
        scratch_shapes=[pltpu.VMEM((tm, D_MODEL), BF16)],
        compiler_params=_cparams(("parallel", "arbitrary")),
        name="in_proj",
    )(x2d, sc, sh, norm_g.reshape(1, -1), w_in_bf, qg, kg, bd)


_TQ = 512


def _lam(lq_ref, lam_init):
    lq = lq_ref[...]
    a = jnp.sum(lq[0:1] * lq[1:2], axis=-1, keepdims=True)
    b = jnp.sum(lq[2:3] * lq[3:4], axis=-1, keepdims=True)
    return jnp.exp(a) - jnp.exp(b) + lam_init


def _subln(o, sg_ref, lam_init):
    ms = jnp.mean(o * o, axis=-1, keepdims=True)
    return o * lax.rsqrt(ms + 1e-6) * sg_ref[...] * (1.0 - lam_init)


def _split_q(q):
    lane = lax.broadcasted_iota(jnp.int32, q.shape, 1)
    qs = q * jnp.asarray(HEAD_DIM_A ** -0.5, BF16)
    zero = jnp.zeros_like(qs)
    return jnp.where(lane < HEAD_DIM_A, qs, zero), jnp.where(lane >= HEAD_DIM_A, qs, zero)


def _attn_prompt_kernel(slopes_ref, q_ref, k_ref, v_ref, lq_ref, sg_ref, o_ref, kb, vb, *, lam_init):
    h = pl.program_id(1)
    i = pl.program_id(2)
    tq = _TQ

    @pl.when(i == 0)
    def _():
        kb[...] = k_ref[...].astype(BF16)
        vb[:, :LANES] = v_ref[...].astype(BF16)
        vb[:, LANES:] = jnp.ones((vb.shape[0], LANES), BF16)

    slope = slopes_ref[h]
    qc = _split_q(q_ref[...])
    iq = lax.broadcasted_iota(jnp.int32, (tq, tq), 0)
    ik = lax.broadcasted_iota(jnp.int32, (tq, tq), 1)
    rel = (ik - iq).astype(F32)
    bias_off = slope * rel
    visible = (ik // CHUNK) <= (iq // CHUNK)
    bias_diag = jnp.where(visible, -slope * jnp.abs(rel), NEG)

    def block(j, bias, cst, carry):
        kj = kb[pl.ds(pl.multiple_of(j * tq, tq), tq), :]
        vj = vb[pl.ds(pl.multiple_of(j * tq, tq), tq), :]
        s = [_dot_nt(qc[c], kj) + bias for c in range(2)]
        m_new = [jnp.maximum(carry[c][0], jnp.max(s[c], axis=-1, keepdims=True) + cst) for c in range(2)]
        p = [jnp.exp(s[c] - (m_new[c] - cst)).astype(BF16) for c in range(2)]
        pv = [_dot(p[c], vj) for c in range(2)]
        return tuple((m_new[c], carry[c][1] * jnp.exp(carry[c][0] - m_new[c]) + pv[c]) for c in range(2))

    init = tuple((jnp.full((tq, 1), NEG, F32), jnp.zeros((tq, 2 * LANES), F32)) for _ in range(2))

    def off_body(j, carry):
        cst = -slope * ((i - j) * tq).astype(F32)
        return block(j, bias_off, cst, carry)

    carry = lax.fori_loop(0, i, off_body, init)
    carry = block(i, bias_diag, jnp.zeros((), F32), carry)

    lam = _lam(lq_ref, lam_init)
    o = [acc[:, :LANES] / acc[:, LANES:LANES + 1] for (_, acc) in carry]
    o_ref[...] = _subln(o[0] - lam * o[1], sg_ref, lam_init)


def _attn_prompt(q, k, v, slopes, lambda_qk, subln_g, batch, seq, lam_init):
    n = batch * seq
    nq = seq // _TQ
    kern = functools.partial(_attn_prompt_kernel, lam_init=lam_init)
    return pl.pallas_call(
        kern,
        grid=(batch, N_HEADS_A, nq),
        in_specs=[
            pl.BlockSpec(memory_space=pltpu.SMEM),
            pl.BlockSpec((_TQ, LANES), lambda b, h, i: (b * nq + i, h)),
            pl.BlockSpec((seq, LANES), lambda b, h, i: (b, h)),
            pl.BlockSpec((seq, LANES), lambda b, h, i: (b, h)),
            pl.BlockSpec((4, HEAD_DIM_A), lambda b, h, i: (0, 0)),
            pl.BlockSpec((1, LANES), lambda b, h, i: (0, 0)),
        ],
        out_specs=pl.BlockSpec((_TQ, LANES), lambda b, h, i: (b * nq + i, h)),
        out_shape=jax.ShapeDtypeStruct((n, W_A), F32),
        scratch_shapes=[pltpu.VMEM((seq, LANES), BF16), pltpu.VMEM((seq, 2 * LANES), BF16)],
        compiler_params=_cparams(("parallel", "parallel", "arbitrary")),
        name="attn_prompt",
    )(slopes, q, k, v, lambda_qk, subln_g.reshape(1, -1))


def _attn_sample_kernel(slopes_ref, q_ref, kp_ref, vp_ref, kn_ref, vn_ref, lq_ref, sg_ref, o_ref, *, lam_init, past):
    t = q_ref.shape[0]
    lam = _lam(lq_ref, lam_init)

    def bias(nk, k0, slope):
        qpos = past + lax.broadcasted_iota(jnp.int32, (t, nk), 0)
        kpos = k0 + lax.broadcasted_iota(jnp.int32, (t, nk), 1)
        vis = (kpos // CHUNK) <= (qpos // CHUNK)
        return jnp.where(vis, -slope * jnp.abs(qpos - kpos).astype(F32), NEG)

    for h in range(N_HEADS_A):
        sl = slice(h * LANES, (h + 1) * LANES)
        slope = slopes_ref[h]
        qc = _split_q(q_ref[:, sl])
        kp = kp_ref[0, :, h, :].astype(BF16)
        vp = vp_ref[0, :, h, :].astype(BF16)
        kn = kn_ref[:, sl].astype(BF16)
        vn = vn_ref[:, sl].astype(BF16)
        bp = bias(past, 0, slope)
        bn = bias(t, past, slope)
        o = []
        for c in range(2):
            sp = _dot_nt(qc[c], kp) + bp
            sn = _dot_nt(qc[c], kn) + bn
            m = jnp.maximum(jnp.max(sp, axis=-1, keepdims=True), jnp.max(sn, axis=-1, keepdims=True))
            pp = jnp.exp(sp - m)
            pn = jnp.exp(sn - m)
            l = jnp.sum(pp, axis=-1, keepdims=True) + jnp.sum(pn, axis=-1, keepdims=True)
            o.append((_dot(pp.astype(BF16), vp) + _dot(pn.astype(BF16), vn)) / l)
        o_ref[:, sl] = _subln(o[0] - lam * o[1], sg_ref, lam_init)


def _attn_sample(q, k_new, v_new, k_past, v_past, slopes, lambda_qk, subln_g, batch, t, lam_init):
    past = k_past.shape[1]
    kern = functools.partial(_attn_sample_kernel, lam_init=lam_init, past=past)
    tok = pl.BlockSpec((t, W_A), lambda b: (b, 0))
    cache = pl.BlockSpec((1, past, N_HEADS_A, LANES), lambda b: (b, 0, 0, 0))
    return pl.pallas_call(
        kern,
        grid=(batch,),
        in_specs=[
            pl.BlockSpec(memory_space=pltpu.SMEM),
            tok, cache, cache, tok, tok,
            pl.BlockSpec((4, HEAD_DIM_A), lambda b: (0, 0)),
            pl.BlockSpec((1, LANES), lambda b: (0, 0)),
        ],
        out_specs=tok,
        out_shape=jax.ShapeDtypeStruct((batch * t, W_A), F32),
        compiler_params=_cparams(("parallel",)),
        name="attn_sample",
    )(slopes, q, k_past, v_past, k_new, v_new, lambda_qk, subln_g.reshape(1, -1))


_N_PAIR = N_HEADS_B // 2
_XW_COL = 3 * W_B
_XG_COL = 3 * W_B + LANES
_GRP = 8


def _seg_cumprod(x, reverse=False):
    n = x.shape[0]
    ri = lax.broadcasted_iota(jnp.int32, x.shape, 0) % _GRP
    sh = 1
    while sh < _GRP:
        if reverse:
            x = x * jnp.where(ri + sh < _GRP, pltpu.roll(x, n - sh, 0), 1.0)
        else:
            x = x * jnp.where(ri >= sh, pltpu.roll(x, sh, 0), 1.0)
        sh *= 2
    return x


def _stack_heads(x):
    pad = HEAD_SIZE_B - x.shape[0]
    lane = lax.broadcasted_iota(jnp.int32, x.shape, 1)
    parts = []
    for hh in range(2):
        parts.append(jnp.where((lane >= HEAD_SIZE_B) == (hh == 1), x, 0.0))
        if pad:
            parts.append(jnp.zeros((pad, LANES), F32))
    return jnp.concatenate(parts, axis=0).astype(BF16)


def _rwkv_kernel(pr_ref, prev_ref, s0_ref, mu_ref, w0_ref, wd_ref, a0_ref, wa_ref, wg_ref, kk_ref, ka_ref, rk_ref,
                 lg_ref, lb_ref, bd_ref, o4_ref, eye_ref, mle_ref, mlt_ref,
                 ob_ref, sout_ref,
                 s_s, prev_s, w_s, kk_s, b_s, g_s, bon_s, vt_s, ps_s, sab_s, y0_s, r2_s, kd2_s, wb_s, wk_s):
    c = pl.program_id(1)
    nb, tc, _ = pr_ref.shape
    n_pair = nb * _N_PAIR

    @pl.when(c == 0)
    def _():
        prev_s[...] = prev_ref[...]
        s_s[...] = s0_ref[...].reshape(s_s.shape)

    bd = bd_ref[...]
    mle = mle_ref[...]
    mlt = mlt_ref[...]
    for bb in range(nb):
        pr = pr_ref[bb]
        row = lax.broadcasted_iota(jnp.int32, pr.shape, 0)
        shifted = jnp.where(row == 0, prev_s[bb], pltpu.roll(pr, 1, 0))
        prev_s[bb] = pr[tc - 1:tc, :]
        xs = pr + (shifted - pr) * mu_ref[...]
        r = xs[:, 0:W_B]
        k = xs[:, W_B:2 * W_B]
        v = xs[:, 2 * W_B:3 * W_B]
        xwa = xs[:, _XW_COL:_XW_COL + LANES]
        xg = xs[:, _XG_COL:_XG_COL + LANES]
        z = w0_ref[...] + _dot(jnp.tanh(xwa).astype(BF16), wd_ref[...])
        wlog = -(jnp.maximum(-z, 0.0) + jnp.log(1.0 + jnp.exp(-jnp.abs(z)))) - 0.5
        decay = jnp.exp(-jnp.exp(wlog))
        a = jax.nn.sigmoid(a0_ref[...] + _dot(xwa.astype(BF16), wa_ref[...]))
        g_s[bb] = _dot(jax.nn.sigmoid(xg).astype(BF16), wg_ref[...])
        kkraw = k * kk_ref[...]
        k_mod = k * (1.0 + (a - 1.0) * ka_ref[...])
        rkr = r * k_mod * rk_ref[...]
        cum = _seg_cumprod(decay)
        tail = _seg_cumprod(decay, reverse=True) / decay
        rc = r * cum
        ki = k_mod / cum
        kd = k_mod * tail
        w_s[bb] = decay
        pairs = range(_N_PAIR)
        cols = [slice(p * LANES, (p + 1) * LANES) for p in pairs]
        ss = [_dot((kkraw[:, sl] * kkraw[:, sl]).astype(BF16), bd) for sl in cols]
        bons = [_dot(rkr[:, sl].astype(BF16), bd) for sl in cols]
        vts = [_dot_nt(eye_ref[...], _stack_heads(v[:, sl])).astype(BF16) for sl in cols]
        kkn = [kkraw[:, sl] / jnp.maximum(jnp.sqrt(s), 1e-12) for sl, s in zip(cols, ss)]
        bq = [kn * a[:, sl] for sl, kn in zip(cols, kkn)]
        r2 = [_stack_heads(rc[:, sl]) for sl in cols]
        k2 = [_stack_heads(ki[:, sl]) for sl in cols]
        gb = [_dot_nt(_stack_heads(bq[p] / cum[:, cols[p]]), r2[p]) for p in pairs]
        gk = [_dot_nt(k2[p], r2[p]) for p in pairs]
        gg = [_dot_nt(k2[p], _stack_heads(kkn[p] * cum[:, cols[p]] / decay[:, cols[p]])) for p in pairs]
        pss = [_dot(vts[p], (gg[p] * mlt).astype(BF16)) for p in pairs]
        for p in pairs:
            sl = cols[p]
            q = bb * _N_PAIR + p
            kk_s[bb, :, sl] = kkn[p]
            b_s[bb, :, sl] = bq[p]
            bon_s[bb, :, sl] = bons[p] * v[:, sl]
            vt_s[q] = vts[p]
            r2_s[q] = r2[p]
            kd2_s[q] = _stack_heads(kd[:, sl])
            wb_s[q] = (-(gb[p] * mle)).astype(BF16)
            wk_s[q] = (gk[p] * mle).astype(BF16)
            ps_s[q] = pss[p]
    sab_s[...] = jnp.zeros_like(sab_s)
    y0_s[...] = jnp.zeros_like(y0_s)

    ones4 = o4_ref[...]
    lane_t = lax.broadcasted_iota(jnp.int32, (HEAD_SIZE_B, LANES), 1) % HEAD_SIZE_B
    row_g = (lax.broadcasted_iota(jnp.int32, (LANES, LANES), 0) % HEAD_SIZE_B) // _GRP
    sls = [slice(p * LANES, (p + 1) * LANES) for p in range(_N_PAIR)]
    n_half = _N_PAIR // 2
    zero_bf = jnp.zeros((), BF16)

    def stack(mats):
        return jnp.concatenate([jnp.concatenate([mats[2 * q], mats[2 * q + 1]], axis=1) for q in range(len(mats) // 2)], axis=0)

    def unstack(m, j):
        q, o = divmod(j, 2)
        return m[q * HEAD_SIZE_B:(q + 1) * HEAD_SIZE_B, o * LANES:(o + 1) * LANES]

    def group(gi, carry):
        base = pl.multiple_of(gi * _GRP, _GRP)
        rows = lambda ref: [ref[bb, pl.ds(base, _GRP), :] for bb in range(nb)]
        kk8, w8, b8 = rows(kk_s), rows(w_s), rows(b_s)
        row = lambda tiles, p, i: tiles[p // _N_PAIR][i:i + 1, sls[p % _N_PAIR]]
        in_group = row_g == gi
        st = [s_s[p] for p in range(n_pair)]
        for p in range(n_pair):
            y0_s[p] += _dot_nt(st[p].astype(BF16), jnp.where(in_group, r2_s[p], zero_bf))
        for i in range(_GRP):
            tmask = lane_t == (base + i)
            for half in range(2 * nb):
                ps = list(range(half * n_half, (half + 1) * n_half))
                lhs = [(st[p] * row(kk8, p, i) + jnp.where(tmask, ps_s[p], 0.0)).astype(BF16) for p in ps]
                sa = _dot(stack(lhs), ones4)
                for j, p in enumerate(ps):
                    sa_p = unstack(sa, j)
                    sab_s[p] = jnp.where(tmask, sa_p, sab_s[p])
                    st[p] = st[p] * row(w8, p, i) - sa_p * row(b8, p, i)
        for p in range(n_pair):
            s_s[p] = st[p] + _dot(vt_s[p], jnp.where(in_group, kd2_s[p], zero_bf))
        return carry

    lax.fori_loop(0, tc // _GRP, group, 0)

    et = eye_ref[:tc]
    inv_n = 1.0 / HEAD_SIZE_B
    qs = range(n_pair)
    ybp = [y0_s[q] + _dot(sab_s[q].astype(BF16), wb_s[q]) + _dot(vt_s[q], wk_s[q]) for q in qs]
    hi = [x.astype(BF16).astype(F32) for x in ybp]
    ys = [_dot_nt(et, _stack_heads(hi[q])) + _dot_nt(et, _stack_heads(ybp[q] - hi[q])) for q in qs]
    ds = [ys[q] - _dot(ys[q].astype(BF16), bd) * inv_n for q in qs]
    var = [_dot((d * d).astype(BF16), bd) * inv_n for d in ds]
    for q in qs:
        bb, p = divmod(q, _N_PAIR)
        yn = ds[q] * lax.rsqrt(var[q] + GN_EPS) * lg_ref[p] + lb_ref[p]
        ob_ref[bb, :, sls[p]] = (yn + bon_s[bb, :, sls[p]]) * g_s[bb, :, sls[p]]

    @pl.when(c == pl.num_programs(1) - 1)
    def _():
        sout_ref[...] = s_s[...].reshape(sout_ref.shape)


_RWKV_NB = 2


def _rwkv(pr, prev_row, s0_pairs, prm, batch, seq, tc):
    n = batch * seq
    nt = seq // tc
    nb = _RWKV_NB
    row = lambda a: a.reshape(1, -1).astype(F32)
    zeros = jnp.zeros((HEAD_SIZE_B, W_B), F32)
    wd = jnp.concatenate([prm['decay_lora_up'], zeros], axis=0).astype(BF16)
    wa = jnp.concatenate([zeros, prm['iclr_lora_up']], axis=0).astype(BF16)
    wg = prm['gate_lora_up'].astype(BF16)
    ones_h = jnp.ones((HEAD_SIZE_B, HEAD_SIZE_B), F32)
    bd = jnp.kron(jnp.eye(2, dtype=F32), ones_h).astype(BF16)
    ones4 = jnp.kron(jnp.eye(4, dtype=F32), ones_h).astype(BF16)
    eye2 = jnp.concatenate([jnp.eye(HEAD_SIZE_B, dtype=F32)] * 2, axis=1).astype(BF16)
    idx = jnp.arange(LANES)
    hd, tt = idx // HEAD_SIZE_B, idx % HEAD_SIZE_B
    same = (hd[:, None] == hd[None, :]) & ((tt[:, None] // _GRP) == (tt[None, :] // _GRP))
    mle = (same & (tt[:, None] <= tt[None, :])).astype(F32)
    mlt = (same & (tt[:, None] < tt[None, :])).astype(F32)
    lg = prm['lnx_g'].reshape(_N_PAIR, 1, LANES)
    lb = prm['lnx_b'].reshape(_N_PAIR, 1, LANES)
    full = lambda shape: pl.BlockSpec(shape, lambda b, c: (0,) * len(shape))
    vec = full((1, W_B))
    sq = full((LANES, LANES))
    state_spec = pl.BlockSpec((nb, _N_PAIR, HEAD_SIZE_B, LANES), lambda b, c: (b, 0, 0, 0))
    pair_f32 = pltpu.VMEM((nb * _N_PAIR, HEAD_SIZE_B, LANES), F32)
    pair_sq = pltpu.VMEM((nb * _N_PAIR, LANES, LANES), BF16)
    o_b, s_new = pl.pallas_call(
        _rwkv_kernel,
        grid=(batch // nb, nt),
        in_specs=[
            pl.BlockSpec((nb, tc, C_SHIFT), lambda b, c: (b, c, 0)),
            pl.BlockSpec((nb, 1, C_SHIFT), lambda b, c: (b, 0, 0)),
            state_spec,
            full((1, C_SHIFT)), vec, full((LANES, W_B)), vec, full((LANES, W_B)), full((LANES, W_B)),
            vec, vec, vec,
            full((_N_PAIR, 1, LANES)), full((_N_PAIR, 1, LANES)),
            sq, full((2 * LANES, 2 * LANES)), full((HEAD_SIZE_B, LANES)), sq, sq,
        ],
        out_specs=[
            pl.BlockSpec((nb, tc, W_B), lambda b, c: (b, c, 0)),
            state_spec,
        ],
        out_shape=[
            jax.ShapeDtypeStruct((batch, seq, W_B), F32),
            jax.ShapeDtypeStruct((batch, _N_PAIR, HEAD_SIZE_B, LANES), F32),
        ],
        scratch_shapes=[
            pair_f32,
            pltpu.VMEM((nb, 1, C_SHIFT), F32),
        ] + [pltpu.VMEM((nb, tc, W_B), F32)] * 5 + [
            pltpu.VMEM((nb * _N_PAIR, HEAD_SIZE_B, LANES), BF16),
            pair_f32, pair_f32, pair_f32,
            pair_sq, pair_sq, pair_sq, pair_sq,
        ],
        compiler_params=_cparams(("parallel", "arbitrary")),
        name="rwkv",
    )(pr.reshape(batch, seq, C_SHIFT), prev_row, s0_pairs, row(prm['shift_mu']), row(prm['decay_w0']), wd, row(prm['iclr_a0']),
      wa, wg, row(prm['k_k']), row(prm['k_a']), row(prm['r_k']), lg, lb, bd, ones4, eye2, mle, mlt)
    return o_b.reshape(n, W_B), s_new


def _merge_kernel(oa_ref, ob_ref, g_ref, x_ref, gt1_ref, sc2_ref, sh2_ref, ng_ref, w_ref, x1_ref, h2_ref):
    g = g_ref[...].astype(F32)
    mixed = (g[:, :D_MODEL] * oa_ref[...] + g[:, D_MODEL:] * ob_ref[...]).astype(BF16)
    x1 = x_ref[...] + gt1_ref[0] * _dot(mixed, w_ref[...])
    x1_ref[...] = x1
    ms = jnp.mean(x1 * x1, axis=-1, keepdims=True)
    y = x1 * lax.rsqrt(ms + 1e-6) * ng_ref[...]
    h2_ref[...] = (y * (1.0 + sc2_ref[0]) + sh2_ref[0]).astype(BF16)


def _merge(o_a, o_b, gates, x2d, gt1, sc2, sh2, norm_g, w_out_bf, tm, rows_per_mod):
    n = x2d.shape[0]
    r = gt1.shape[1]
    rowblk = pl.BlockSpec((tm, D_MODEL), lambda i: (i, 0))
    mod_spec = pl.BlockSpec((1, r, D_MODEL), lambda i: (i // rows_per_mod, 0, 0))
    return pl.pallas_call(
        _merge_kernel,
        grid=(n // tm,),
        in_specs=[
            rowblk, rowblk,
            pl.BlockSpec((tm, 2 * D_MODEL), lambda i: (i, 0)),
            rowblk, mod_spec, mod_spec, mod_spec,
            pl.BlockSpec((1, D_MODEL), lambda i: (0, 0)),
            pl.BlockSpec((D_MODEL, D_MODEL), lambda i: (0, 0)),
        ],
        out_specs=[rowblk, rowblk],
        out_shape=[jax.ShapeDtypeStruct((n, D_MODEL), F32), jax.ShapeDtypeStruct((n, D_MODEL), BF16)],
        compiler_params=_cparams(("parallel",)),
        name="merge",
    )(o_a, o_b, gates, x2d, gt1, sc2, sh2, norm_g.reshape(1, -1), w_out_bf)


_NTOP = PEER_TOPK + 1
_CAND = [(a, b) for a in range(_NTOP) for b in range(_NTOP) if (a + 1) * (b + 1) <= _NTOP]


def _peer_prep_kernel(h2_ref, wq_ref, keys_ref, s2_ref, e2_ref, th_ref, cc_ref, s1_s, s2_s, v1_s, v2_s):
    h2 = h2_ref[...]
    ninf = -jnp.inf

    def head(h, carry):
        for c, (s_s, v_s) in enumerate(((s1_s, v1_s), (s2_s, v2_s))):
            qh = _dot(h2, wq_ref[2 * h + c]).astype(BF16)
            s = _dot_nt(keys_ref[2 * h + c], qh)
            s_s[h] = s
            cur = s
            for a in range(_NTOP):
                m = jnp.max(cur, axis=0, keepdims=True)
                v_s[h, a] = m
                cur = jnp.where(cur == m, ninf, cur)
        return carry

    lax.fori_loop(0, PEER_HEADS, head, 0)

    v1 = [jnp.concatenate([v1_s[h, a] for h in range(PEER_HEADS)], axis=0) for a in range(_NTOP)]
    v2 = [jnp.concatenate([v2_s[h, a] for h in range(PEER_HEADS)], axis=0) for a in range(_NTOP)]
    cand = [v1[a] + v2[b] for (a, b) in _CAND]
    tops = []
    for _ in range(_NTOP):
        m = cand[0]
        for cv in cand[1:]:
            m = jnp.maximum(m, cv)
        tops.append(m)
        rem = jnp.ones_like(m)
        nxt = []
        for cv in cand:
            hit = jnp.where(cv == m, rem, 0.0)
            nxt.append(jnp.where(hit > 0.0, ninf, cv))
            rem = rem - hit
        cand = nxt
    tau = jnp.where(tops[PEER_TOPK] == ninf, tops[PEER_TOPK - 1], 0.5 * (tops[PEER_TOPK - 1] + tops[PEER_TOPK]))
    z = jnp.exp(tops[0] - tops[0])
    for a in range(1, PEER_TOPK):
        z = z + jnp.exp(tops[a] - tops[0])
    rz = 1.0 / z
    for h in range(PEER_HEADS):
        s1 = s1_s[h]
        s2 = s2_s[h]
        s2_ref[h] = s2
        e2_ref[h] = jnp.exp(s2 - v2[0][h:h + 1]).astype(BF16)
        th_ref[h] = tau[h:h + 1] - s1
        cc_ref[h] = (jnp.exp(s1 - v1[0][h:h + 1]) * rz[h:h + 1]).astype(BF16)


def _peer_prep(h2, wq_heads, keys_bf, tb):
    n = h2.shape[0]
    out = jax.ShapeDtypeStruct((PEER_HEADS, N_KEYS, n), F32)
    out_bf = jax.ShapeDtypeStruct((PEER_HEADS, N_KEYS, n), BF16)
    ospec = pl.BlockSpec((PEER_HEADS, N_KEYS, tb), lambda i: (0, 0, i))
    return pl.pallas_call(
        _peer_prep_kernel,
        grid=(n // tb,),
        in_specs=[
            pl.BlockSpec((tb, D_MODEL), lambda i: (i, 0)),
            pl.BlockSpec((2 * PEER_HEADS, D_MODEL, N_KEYS), lambda i: (0, 0, 0)),
            pl.BlockSpec((2 * PEER_HEADS, N_KEYS, N_KEYS), lambda i: (0, 0, 0)),
        ],
        out_specs=[ospec] * 4,
        out_shape=[out, out_bf, out, out_bf],
        scratch_shapes=[
            pltpu.VMEM((PEER_HEADS, N_KEYS, tb), F32), pltpu.VMEM((PEER_HEADS, N_KEYS, tb), F32),
            pltpu.VMEM((PEER_HEADS, _NTOP, 1, tb), F32), pltpu.VMEM((PEER_HEADS, _NTOP, 1, tb), F32),
        ],
        compiler_params=_cparams(("parallel",)),
        name="peer_prep",
    )(h2, wq_heads, keys_bf)


_TE = 1024
_ROWS = _TE // N_KEYS


def _peer_dense_kernel(h2_ref, u_ref, vt_ref, s2_ref, e2_ref, th_ref, cc_ref, x1_ref, gt2_ref, o_ref, acc, p_s):
    e = pl.program_id(1)

    @pl.when(e == 0)
    def _():
        acc[...] = jnp.zeros_like(acc)

    base = pl.multiple_of(e * _ROWS, _ROWS)
    th8 = [th_ref[h, pl.ds(base, _ROWS), :] for h in range(PEER_HEADS)]
    cc8 = [cc_ref[h, pl.ds(base, _ROWS), :] for h in range(PEER_HEADS)]
    ht = _dot_nt(u_ref[...], h2_ref[...])
    for r in range(_ROWS):
        rs = slice(r * N_KEYS, (r + 1) * N_KEYS)
        g = None
        for h in range(PEER_HEADS):
            w = jnp.where(s2_ref[h] >= th8[h][r:r + 1], e2_ref[h] * cc8[h][r:r + 1], jnp.zeros((), BF16))
            g = w if g is None else g + w
        x = ht[rs]
        gelu = 0.5 * x * (1.0 + lax.erf(x * (2.0 ** -0.5)))
        p_s[rs, :] = g * gelu.astype(BF16)
    acc[...] += _dot(vt_ref[...], p_s[...])

    @pl.when(e == pl.num_programs(1) - 1)
    def _():
        o_ref[...] = x1_ref[...] + gt2_ref[0] * acc[...].T


def _peer_dense(h2, u_bf, vt_bf, s2, e2, th, cc, x1, gt2, tb, rows_per_mod):
    n = h2.shape[0]
    ne = u_bf.shape[0] // _TE
    r = gt2.shape[1]
    tok = pl.BlockSpec((PEER_HEADS, N_KEYS, tb), lambda i, e: (0, 0, i))
    rowblk = pl.BlockSpec((tb, D_MODEL), lambda i, e: (i, 0))
    return pl.pallas_call(
        _peer_dense_kernel,
        grid=(n // tb, ne),
        in_specs=[
            rowblk,
            pl.BlockSpec((_TE, D_MODEL), lambda i, e: (e, 0)),
            pl.BlockSpec((D_MODEL, _TE), lambda i, e: (0, e)),
            tok, tok, tok, tok,
            rowblk,
            pl.BlockSpec((1, r, D_MODEL), lambda i, e: (i // rows_per_mod, 0, 0)),
        ],
        out_specs=rowblk,
        out_shape=jax.ShapeDtypeStruct((n, D_MODEL), F32),
        scratch_shapes=[pltpu.VMEM((D_MODEL, tb), F32), pltpu.VMEM((_TE, tb), BF16)],
        compiler_params=_cparams(("parallel", "arbitrary")),
        name="peer_dense",
    )(h2, u_bf, vt_bf, s2, e2, th, cc, x1, gt2)


def _state_to_pairs(s):
    b = s.shape[0]
    return s.reshape(b, _N_PAIR, 2, HEAD_SIZE_B, HEAD_SIZE_B).transpose(0, 1, 3, 2, 4).reshape(b, _N_PAIR, HEAD_SIZE_B, LANES)


def _pairs_to_state(s):
    b = s.shape[0]
    return s.reshape(b, _N_PAIR, HEAD_SIZE_B, 2, HEAD_SIZE_B).transpose(0, 1, 3, 2, 4).reshape(b, N_HEADS_B, HEAD_SIZE_B, HEAD_SIZE_B)


def _stream(x, mods, w, prm, past, cfg):
    batch, t, _ = x.shape
    n = batch * t
    tm, rpm, tc, tb, rpm_tb = cfg
    sh1, sc1, gt1, sh2, sc2, gt2 = mods
    x2d = x.reshape(n, D_MODEL)
    q, k, v, pr, gates = _in_proj(x2d, sc1, sh1, prm['norm_mix_g'], w['w_in'], prm['q_norm_g'], prm['k_norm_g'], tm, rpm)
    if past is None:
        o_a = _attn_prompt(q, k, v, w['slopes'], prm['lambda_qk'], prm['subln_g'], batch, t, w['lam_init'])
        prev = jnp.zeros((batch, 1, C_SHIFT), F32)
        s0 = jnp.zeros((batch, _N_PAIR, HEAD_SIZE_B, LANES), F32)
    else:
        k_past, v_past, s_past, prev_row = past
        o_a = _attn_sample(q, k, v, k_past, v_past, w['slopes'],
                           prm['lambda_qk'], prm['subln_g'], batch, t, w['lam_init'])
        prev = prev_row.reshape(batch, 1, C_SHIFT)
        s0 = _state_to_pairs(s_past)
    o_b, s_pairs = _rwkv(pr, prev, s0, prm, batch, t, tc)
    x1, h2 = _merge(o_a, o_b, gates, x2d, gt1, sc2, sh2, prm['norm_ffn_g'], w['w_out'], tm if tm <= 512 else 512,
                    rpm if tm <= 512 else rpm * (tm // 512))
    s2, e2, th, cc = _peer_prep(h2, w['wq'], w['keys'], tb)
    y = _peer_dense(h2, w['u'], w['vt'], s2, e2, th, cc, x1, gt2, tb, rpm_tb)
    new_k = k.reshape(batch, t, N_HEADS_A, 2 * HEAD_DIM_A)
    new_v = v.reshape(batch, t, N_HEADS_A, 2 * HEAD_DIM_A)
    shift = pr.reshape(batch, t, C_SHIFT)[:, -1]
    return y.reshape(batch, t, D_MODEL), (new_k, new_v, _pairs_to_state(s_pairs), shift)


def kernel(x_prompt, x_sample, c_prompt, c_sample, cache_attn_k, cache_attn_v, state_rwkv_wkv, state_rwkv_shift, norm_mix_g, norm_ffn_g, w_ada, b_ada, w_in, q_norm_g, k_norm_g, lambda_qk, subln_g, shift_mu, decay_w0, decay_lora_up, iclr_a0, iclr_lora_up, gate_lora_up, k_k, k_a, r_k, lnx_g, lnx_b, w_out, peer_w_query, peer_sub_keys, peer_u, peer_v):
    depth = w_in.shape[0]
    bp, tp, _ = x_prompt.shape
    bs, ts, _ = x_sample.shape
    slopes = jnp.asarray([2.0 ** (-8.0 * (h + 1) / N_HEADS_A) for h in range(N_HEADS_A)], F32)
    st_p, st_s = [], []
    for l in range(depth):
        prm = {
            'norm_mix_g': norm_mix_g[l], 'norm_ffn_g': norm_ffn_g[l], 'q_norm_g': q_norm_g[l], 'k_norm_g': k_norm_g[l],
            'lambda_qk': lambda_qk[l], 'subln_g': subln_g[l], 'shift_mu': shift_mu[l], 'decay_w0': decay_w0[l],
            'decay_lora_up': decay_lora_up[l], 'iclr_a0': iclr_a0[l], 'iclr_lora_up': iclr_lora_up[l],
            'gate_lora_up': gate_lora_up[l], 'k_k': k_k[l], 'k_a': k_a[l], 'r_k': r_k[l], 'lnx_g': lnx_g[l], 'lnx_b': lnx_b[l],
        }
        w = {
            'w_in': w_in[l].astype(BF16),
            'w_out': w_out[l].astype(BF16),
            'wq': peer_w_query[l].reshape(D_MODEL, 2 * PEER_HEADS, N_KEYS).transpose(1, 0, 2).astype(BF16),
            'keys': peer_sub_keys[l].reshape(2 * PEER_HEADS, N_KEYS, N_KEYS).astype(BF16),
            'u': peer_u[l].astype(BF16),
            'vt': peer_v[l].T.astype(BF16),
            'slopes': slopes,
            'lam_init': 0.8 - 0.6 * math.exp(-0.3 * l),
        }
        ada = _ada(jnp.concatenate([c_prompt, c_sample], axis=0), w_ada[l], b_ada[l])
        mods = jnp.split(ada, 6, axis=-1)
        mods_p = [m[:bp].reshape(bp, 1, D_MODEL) for m in mods]
        mods_s = [jnp.repeat(m[bp:], ts, axis=0).reshape(1, bs * ts, D_MODEL) for m in mods]
        x_prompt, sp = _stream(x_prompt, mods_p, w, prm, None, (2048, tp // 2048, CHUNK, 512, tp // 512))
        x_sample, ss = _stream(x_sample, mods_s, w, prm,
                               (cache_attn_k[l], cache_attn_v[l], state_rwkv_wkv[l], state_rwkv_shift[l]),
                               (bs * ts, 1, ts, bs * ts, 1))
        st_p.append(sp)
        st_s.append(ss)
    stack = lambda sts, i: jnp.stack([s[i] for s in sts])
    return (x_prompt, x_sample, stack(st_p, 0), stack(st_p, 1), stack(st_p, 2), stack(st_p, 3),
            stack(st_s, 0), stack(st_s, 1), stack(st_s, 2), stack(st_s, 3))
```

```python
import functools
import math

import jax
import jax.numpy as jnp
from jax import lax
from jax.experimental import pallas as pl
from jax.experimental.pallas import tpu as pltpu

F32 = jnp.float32
BF16 = jnp.bfloat16

D_MODEL = 1024
N_HEADS_A = 8
HEAD_DIM_A = 64
W_A = 1024
N_HEADS_B = 16
HEAD_SIZE_B = 64
W_B = 1024
C_SHIFT = 3328
D_IN = 8448
CHUNK = 64
GN_EPS = 64e-5
N_KEYS = 128
PEER_HEADS = 8
PEER_TOPK = 16
NEG = -1e30
LANES = 128
VMEM_LIMIT = 56 * 1024 * 1024

_NT = (((1,), (1,)), ((), ()))


def _cparams(sem):
    return pltpu.CompilerParams(dimension_semantics=sem, vmem_limit_bytes=VMEM_LIMIT)


def _dot(a, b):
    return jnp.dot(a, b, preferred_element_type=F32)


def _dot_nt(a, b):
    return lax.dot_general(a, b, _NT, preferred_element_type=F32)


def _ada_kernel(c_ref, w_ref, b_ref, o_ref):
    c = c_ref[...]
    s = c * jax.nn.sigmoid(c)
    o_ref[...] = _dot(s.astype(BF16), w_ref[...].astype(BF16)) + b_ref[...]


def _ada(c, w_ada, b_ada):
    n = c.shape[0]
    tn = 1024
    return pl.pallas_call(
        _ada_kernel,
        grid=(6 * D_MODEL // tn,),
        in_specs=[
            pl.BlockSpec((n, D_MODEL), lambda j: (0, 0)),
            pl.BlockSpec((D_MODEL, tn), lambda j: (0, j)),
            pl.BlockSpec((1, tn), lambda j: (0, j)),
        ],
        out_specs=pl.BlockSpec((n, tn), lambda j: (0, j)),
        out_shape=jax.ShapeDtypeStruct((n, 6 * D_MODEL), F32),
        compiler_params=_cparams(("arbitrary",)),
        name="ada",
    )(c, w_ada, b_ada.reshape(1, -1))


_IN_TN = 256
_Q_T, _K_T, _V_T, _PR_T, _G_T = 0, 4, 8, 12, 25
_N_T = D_IN // _IN_TN


def _in_kernel(x_ref, sc_ref, sh_ref, g_ref, w_ref, qg_ref, kg_ref, bd_ref,
               q_ref, k_ref, v_ref, pr_ref, gt_ref, h_scr):
    j = pl.program_id(1)

    @pl.when(j == 0)
    def _():
        x = x_ref[...]
        ms = jnp.mean(x * x, axis=-1, keepdims=True)
        y = x * lax.rsqrt(ms + 1e-6) * g_ref[...]
        h_scr[...] = (y * (1.0 + sc_ref[0]) + sh_ref[0]).astype(BF16)

    acc = _dot(h_scr[...], w_ref[...])

    def head_norm(a, g):
        ms = _dot((a * a).astype(BF16), bd_ref[...])
        return a * lax.rsqrt(ms + 1e-6) * g

    @pl.when(j < _K_T)
    def _():
        q_ref[...] = head_norm(acc, qg_ref[...]).astype(BF16)

    @pl.when((j >= _K_T) & (j < _V_T))
    def _():
        k_ref[...] = head_norm(acc, kg_ref[...])

    @pl.when((j >= _V_T) & (j < _PR_T))
    def _():
        v_ref[...] = acc

    @pl.when((j >= _PR_T) & (j < _G_T))
    def _():
        pr_ref[...] = acc

    @pl.when(j >= _G_T)
    def _():
        gt_ref[...] = jax.nn.sigmoid(acc).astype(BF16)


def _in_proj(x2d, sc, sh, norm_g, w_in_bf, q_g, k_g, tm, rows_per_mod):
    n = x2d.shape[0]
    tn = _IN_TN
    r = sc.shape[1]
    bd = jnp.kron(jnp.eye(tn // HEAD_DIM_A, dtype=F32), jnp.full((HEAD_DIM_A, HEAD_DIM_A), 1.0 / HEAD_DIM_A, F32)).astype(BF16)
    qg = jnp.tile(q_g.reshape(1, -1), (1, tn // HEAD_DIM_A))
    kg = jnp.tile(k_g.reshape(1, -1), (1, tn // HEAD_DIM_A))

    def seg(first, count):
        return lambda i, j: (i, jnp.clip(j - first, 0, count - 1))

    mod_spec = pl.BlockSpec((1, r, D_MODEL), lambda i, j: (i // rows_per_mod, 0, 0))
    return pl.pallas_call(
        _in_kernel,
        grid=(n // tm, _N_T),
        in_specs=[
            pl.BlockSpec((tm, D_MODEL), lambda i, j: (i, 0)),
            mod_spec, mod_spec,
            pl.BlockSpec((1, D_MODEL), lambda i, j: (0, 0)),
            pl.BlockSpec((D_MODEL, tn), lambda i, j: (0, j)),
            pl.BlockSpec((1, tn), lambda i, j: (0, 0)),
            pl.BlockSpec((1, tn), lambda i, j: (0, 0)),
            pl.BlockSpec((tn, tn), lambda i, j: (0, 0)),
        ],
        out_specs=[
            pl.BlockSpec((tm, tn), seg(_Q_T, 4)),
            pl.BlockSpec((tm, tn), seg(_K_T, 4)),
            pl.BlockSpec((tm, tn), seg(_V_T, 4)),
            pl.BlockSpec((tm, tn), seg(_PR_T, 13)),
            pl.BlockSpec((tm, tn), seg(_G_T, 8)),
        ],
        out_shape=[
            jax.ShapeDtypeStruct((n, W_A), BF16),
            jax.ShapeDtypeStruct((n, W_A), F32),
            jax.ShapeDtypeStruct((n, W_A), F32),
            jax.ShapeDtypeStruct((n, C_SHIFT), F32),
            jax.ShapeDtypeStruct((n, 2 * D_MODEL), BF16),
        ],
        scratch_shapes=[pltpu.VMEM((tm, D_MODEL), BF16)],
        compiler_params=_cparams(("parallel", "arbitrary")),
        name="in_proj",
    )(x2d, sc, sh, norm_g.reshape(1, -1), w_in_bf, qg, kg, bd)


_TQ = 512


def _lam(lq_ref, lam_init):
    lq = lq_ref[...]
    a = jnp.sum(lq[0:1] * lq[1:2], axis=-1, keepdims=True)
    b = jnp.sum(lq[2:3] * lq[3:4], axis=-1, keepdims=True)
    return jnp.exp(a) - jnp.exp(b) + lam_init


def _subln(o, sg_ref, lam_init):
    ms = jnp.mean(o * o, axis=-1, keepdims=True)
    return o * lax.rsqrt(ms + 1e-6) * sg_ref[...] * (1.0 - lam_init)


def _split_q(q):
    lane = lax.broadcasted_iota(jnp.int32, q.shape, 1)
    qs = q * jnp.asarray(HEAD_DIM_A ** -0.5, BF16)
    zero = jnp.zeros_like(qs)
    return jnp.where(lane < HEAD_DIM_A, qs, zero), jnp.where(lane >= HEAD_DIM_A, qs, zero)


def _attn_prompt_kernel(slopes_ref, q_ref, k_ref, v_ref, lq_ref, sg_ref, o_ref, kb, vb, *, lam_init):
    h = pl.program_id(1)
    i = pl.program_id(2)
    tq = _TQ

    @pl.when(i == 0)
    def _():
        kb[...] = k_ref[...].astype(BF16)
        vb[:, :LANES] = v_ref[...].astype(BF16)
        vb[:, LANES:] = jnp.ones((vb.shape[0], LANES), BF16)

    slope = slopes_ref[h]
    qc = _split_q(q_ref[...])
    iq = lax.broadcasted_iota(jnp.int32, (tq, tq), 0)
    ik = lax.broadcasted_iota(jnp.int32, (tq, tq), 1)
    rel = (ik - iq).astype(F32)
    bias_off = slope * rel
    visible = (ik // CHUNK) <= (iq // CHUNK)
    bias_diag = jnp.where(visible, -slope * jnp.abs(rel), NEG)

    def block(j, bias, cst, carry):
        kj = kb[pl.ds(pl.multiple_of(j * tq, tq), tq), :]
        vj = vb[pl.ds(pl.multiple_of(j * tq, tq), tq), :]
        s = [_dot_nt(qc[c], kj) + bias for c in range(2)]
        m_new = [jnp.maximum(carry[c][0], jnp.max(s[c], axis=-1, keepdims=True) + cst) for c in range(2)]
        p = [jnp.exp(s[c] - (m_new[c] - cst)).astype(BF16) for c in range(2)]
        pv = [_dot(p[c], vj) for c in range(2)]
        return tuple((m_new[c], carry[c][1] * jnp.exp(carry[c][0] - m_new[c]) + pv[c]) for c in range(2))

    init = tuple((jnp.full((tq, 1), NEG, F32), jnp.zeros((tq, 2 * LANES), F32)) for _ in range(2))

    def off_body(j, carry):
        cst = -slope * ((i - j) * tq).astype(F32)
        return block(j, bias_off, cst, carry)

    carry = lax.fori_loop(0, i, off_body, init)
    carry = block(i, bias_diag, jnp.zeros((), F32), carry)

    lam = _lam(lq_ref, lam_init)
    o = [acc[:, :LANES] / acc[:, LANES:LANES + 1] for (_, acc) in carry]
    o_ref[...] = _subln(o[0] - lam * o[1], sg_ref, lam_init)


def _attn_prompt(q, k, v, slopes, lambda_qk, subln_g, batch, seq, lam_init):
    n = batch * seq
    nq = seq // _TQ
    kern = functools.partial(_attn_prompt_kernel, lam_init=lam_init)
    return pl.pallas_call(
        kern,
        grid=(batch, N_HEADS_A, nq),
        in_specs=[
            pl.BlockSpec(memory_space=pltpu.SMEM),
            pl.BlockSpec((_TQ, LANES), lambda b, h, i: (b * nq + i, h)),
            pl.BlockSpec((seq, LANES), lambda b, h, i: (b, h)),
            pl.BlockSpec((seq, LANES), lambda b, h, i: (b, h)),
            pl.BlockSpec((4, HEAD_DIM_A), lambda b, h, i: (0, 0)),
            pl.BlockSpec((1, LANES), lambda b, h, i: (0, 0)),
        ],
        out_specs=pl.BlockSpec((_TQ, LANES), lambda b, h, i: (b * nq + i, h)),
        out_shape=jax.ShapeDtypeStruct((n, W_A), F32),
        scratch_shapes=[pltpu.VMEM((seq, LANES), BF16), pltpu.VMEM((seq, 2 * LANES), BF16)],
        compiler_params=_cparams(("parallel", "parallel", "arbitrary")),
        name="attn_prompt",
    )(slopes, q, k, v, lambda_qk, subln_g.reshape(1, -1))


def _attn_sample_kernel(slopes_ref, q_ref, kp_ref, vp_ref, kn_ref, vn_ref, lq_ref, sg_ref, o_ref, *, lam_init, past):
    t = q_ref.shape[0]
    lam = _lam(lq_ref, lam_init)

    def bias(nk, k0, slope):
        qpos = past + lax.broadcasted_iota(jnp.int32, (t, nk), 0)
        kpos = k0 + lax.broadcasted_iota(jnp.int32, (t, nk), 1)
        vis = (kpos // CHUNK) <= (qpos // CHUNK)
        return jnp.where(vis, -slope * jnp.abs(qpos - kpos).astype(F32), NEG)

    for h in range(N_HEADS_A):
        sl = slice(h * LANES, (h + 1) * LANES)
        slope = slopes_ref[h]
        qc = _split_q(q_ref[:, sl])
        kp = kp_ref[0, :, h, :].astype(BF16)
        vp = vp_ref[0, :, h, :].astype(BF16)
        kn = kn_ref[:, sl].astype(BF16)
        vn = vn_ref[:, sl].astype(BF16)
        bp = bias(past, 0, slope)
        bn = bias(t, past, slope)
        o = []
        for c in range(2):
            sp = _dot_nt(qc[c], kp) + bp
            sn = _dot_nt(qc[c], kn) + bn
            m = jnp.maximum(jnp.max(sp, axis=-1, keepdims=True), jnp.max(sn, axis=-1, keepdims=True))
            pp = jnp.exp(sp - m)
            pn = jnp.exp(sn - m)
            l = jnp.sum(pp, axis=-1, keepdims=True) + jnp.sum(pn, axis=-1, keepdims=True)
            o.append((_dot(pp.astype(BF16), vp) + _dot(pn.astype(BF16), vn)) / l)
        o_ref[:, sl] = _subln(o[0] - lam * o[1], sg_ref, lam_init)


def _attn_sample(q, k_new, v_new, k_past, v_past, slopes, lambda_qk, subln_g, batch, t, lam_init):
    past = k_past.shape[1]
    kern = functools.partial(_attn_sample_kernel, lam_init=lam_init, past=past)
    tok = pl.BlockSpec((t, W_A), lambda b: (b, 0))
    cache = pl.BlockSpec((1, past, N_HEADS_A, LANES), lambda b: (b, 0, 0, 0))
    return pl.pallas_call(
        kern,
        grid=(batch,),
        in_specs=[
            pl.BlockSpec(memory_space=pltpu.SMEM),
            tok, cache, cache, tok, tok,
            pl.BlockSpec((4, HEAD_DIM_A), lambda b: (0, 0)),
            pl.BlockSpec((1, LANES), lambda b: (0, 0)),
        ],
        out_specs=tok,
        out_shape=jax.ShapeDtypeStruct((batch * t, W_A), F32),
        compiler_params=_cparams(("parallel",)),
        name="attn_sample",
    )(slopes, q, k_past, v_past, k_new, v_new, lambda_qk, subln_g.reshape(1, -1))


_N_PAIR = N_HEADS_B // 2
_XW_COL = 3 * W_B
_XG_COL = 3 * W_B + LANES
_GRP = 8


def _seg_cumprod(x, reverse=False):
    n = x.shape[0]
    ri = lax.broadcasted_iota(jnp.int32, x.shape, 0) % _GRP
    sh = 1
    while sh < _GRP:
        if reverse:
            x = x * jnp.where(ri + sh < _GRP, pltpu.roll(x, n - sh, 0), 1.0)
        else:
            x = x * jnp.where(ri >= sh, pltpu.roll(x, sh, 0), 1.0)
        sh *= 2
    return x


def _stack_heads(x):
    pad = HEAD_SIZE_B - x.shape[0]
    lane = lax.broadcasted_iota(jnp.int32, x.shape, 1)
    parts = []
    for hh in range(2):
        parts.append(jnp.where((lane >= HEAD_SIZE_B) == (hh == 1), x, 0.0))
        if pad:
            parts.append(jnp.zeros((pad, LANES), F32))
    return jnp.concatenate(parts, axis=0).astype(BF16)


def _rwkv_kernel(pr_ref, prev_ref, s0_ref, mu_ref, w0_ref, wd_ref, a0_ref, wa_ref, wg_ref, kk_ref, ka_ref, rk_ref,
                 lg_ref, lb_ref, bd_ref, o4_ref, eye_ref, mle_ref, mlt_ref,
                 ob_ref, sout_ref,
                 s_s, prev_s, w_s, kk_s, b_s, g_s, bon_s, vt_s, ps_s, sab_s, y0_s, r2_s, kd2_s, wb_s, wk_s):
    c = pl.program_id(1)
    nb, tc, _ = pr_ref.shape
    n_pair = nb * _N_PAIR

    @pl.when(c == 0)
    def _():
        prev_s[...] = prev_ref[...]
        s_s[...] = s0_ref[...].reshape(s_s.shape)

    bd = bd_ref[...]
    mle = mle_ref[...]
    mlt = mlt_ref[...]
    for bb in range(nb):
        pr = pr_ref[bb]
        row = lax.broadcasted_iota(jnp.int32, pr.shape, 0)
        shifted = jnp.where(row == 0, prev_s[bb], pltpu.roll(pr, 1, 0))
        prev_s[bb] = pr[tc - 1:tc, :]
        xs = pr + (shifted - pr) * mu_ref[...]
        r = xs[:, 0:W_B]
        k = xs[:, W_B:2 * W_B]
        v = xs[:, 2 * W_B:3 * W_B]
        xwa = xs[:, _XW_COL:_XW_COL + LANES]
        xg = xs[:, _XG_COL:_XG_COL + LANES]
        z = w0_ref[...] + _dot(jnp.tanh(xwa).astype(BF16), wd_ref[...])
        wlog = -(jnp.maximum(-z, 0.0) + jnp.log(1.0 + jnp.exp(-jnp.abs(z)))) - 0.5
        decay = jnp.exp(-jnp.exp(wlog))
        a = jax.nn.sigmoid(a0_ref[...] + _dot(xwa.astype(BF16), wa_ref[...]))
        g_s[bb] = _dot(jax.nn.sigmoid(xg).astype(BF16), wg_ref[...])
        kkraw = k * kk_ref[...]
        k_mod = k * (1.0 + (a - 1.0) * ka_ref[...])
        rkr = r * k_mod * rk_ref[...]
        cum = _seg_cumprod(decay)
        tail = _seg_cumprod(decay, reverse=True) / decay
        rc = r * cum
        ki = k_mod / cum
        kd = k_mod * tail
        w_s[bb] = decay
        pairs = range(_N_PAIR)
        cols = [slice(p * LANES, (p + 1) * LANES) for p in pairs]
        ss = [_dot((kkraw[:, sl] * kkraw[:, sl]).astype(BF16), bd) for sl in cols]
        bons = [_dot(rkr[:, sl].astype(BF16), bd) for sl in cols]
        vts = [_dot_nt(eye_ref[...], _stack_heads(v[:, sl])).astype(BF16) for sl in cols]
        kkn = [kkraw[:, sl] / jnp.maximum(jnp.sqrt(s), 1e-12) for sl, s in zip(cols, ss)]
        bq = [kn * a[:, sl] for sl, kn in zip(cols, kkn)]
        r2 = [_stack_heads(rc[:, sl]) for sl in cols]
        k2 = [_stack_heads(ki[:, sl]) for sl in cols]
        gb = [_dot_nt(_stack_heads(bq[p] / cum[:, cols[p]]), r2[p]) for p in pairs]
        gk = [_dot_nt(k2[p], r2[p]) for p in pairs]
        gg = [_dot_nt(k2[p], _stack_heads(kkn[p] * cum[:, cols[p]] / decay[:, cols[p]])) for p in pairs]
        pss = [_dot(vts[p], (gg[p] * mlt).astype(BF16)) for p in pairs]
        for p in pairs:
            sl = cols[p]
            q = bb * _N_PAIR + p
            kk_s[bb, :, sl] = kkn[p]
            b_s[bb, :, sl] = bq[p]
            bon_s[bb, :, sl] = bons[p] * v[:, sl]
            vt_s[q] = vts[p]
            r2_s[q] = r2[p]
            kd2_s[q] = _stack_heads(kd[:, sl])
            wb_s[q] = (-(gb[p] * mle)).astype(BF16)
            wk_s[q] = (gk[p] * mle).astype(BF16)
            ps_s[q] = pss[p]
    sab_s[...] = jnp.zeros_like(sab_s)
    y0_s[...] = jnp.zeros_like(y0_s)

    ones4 = o4_ref[...]
    lane_t = lax.broadcasted_iota(jnp.int32, (HEAD_SIZE_B, LANES), 1) % HEAD_SIZE_B
    row_g = (lax.broadcasted_iota(jnp.int32, (LANES, LANES), 0) % HEAD_SIZE_B) // _GRP
    sls = [slice(p * LANES, (p + 1) * LANES) for p in range(_N_PAIR)]
    n_half = _N_PAIR // 2
    zero_bf = jnp.zeros((), BF16)

    def stack(mats):
        return jnp.concatenate([jnp.concatenate([mats[2 * q], mats[2 * q + 1]], axis=1) for q in range(len(mats) // 2)], axis=0)

    def unstack(m, j):
        q, o = divmod(j, 2)
        return m[q * HEAD_SIZE_B:(q + 1) * HEAD_SIZE_B, o * LANES:(o + 1) * LANES]

    def group(gi, carry):
        base = pl.multiple_of(gi * _GRP, _GRP)
        rows = lambda ref: [ref[bb, pl.ds(base, _GRP), :] for bb in range(nb)]
        kk8, w8, b8 = rows(kk_s), rows(w_s), rows(b_s)
        row = lambda tiles, p, i: tiles[p // _N_PAIR][i:i + 1, sls[p % _N_PAIR]]
        in_group = row_g == gi
        st = [s_s[p] for p in range(n_pair)]
        for p in range(n_pair):
            y0_s[p] += _dot_nt(st[p].astype(BF16), jnp.where(in_group, r2_s[p], zero_bf))
        for i in range(_GRP):
            tmask = lane_t == (base + i)
            for half in range(2 * nb):
                ps = list(range(half * n_half, (half + 1) * n_half))
                lhs = [(st[p] * row(kk8, p, i) + jnp.where(tmask, ps_s[p], 0.0)).astype(BF16) for p in ps]
                sa = _dot(stack(lhs), ones4)
                for j, p in enumerate(ps):
                    sa_p = unstack(sa, j)
                    sab_s[p] = jnp.where(tmask, sa_p, sab_s[p])
                    st[p] = st[p] * row(w8, p, i) - sa_p * row(b8, p, i)
        for p in range(n_pair):
            s_s[p] = st[p] + _dot(vt_s[p], jnp.where(in_group, kd2_s[p], zero_bf))
        return carry

    lax.fori_loop(0, tc // _GRP, group, 0)

    et = eye_ref[:tc]
    inv_n = 1.0 / HEAD_SIZE_B
    qs = range(n_pair)
    ybp = [y0_s[q] + _dot(sab_s[q].astype(BF16), wb_s[q]) + _dot(vt_s[q], wk_s[q]) for q in qs]
    hi = [x.astype(BF16).astype(F32) for x in ybp]
    ys = [_dot_nt(et, _stack_heads(hi[q])) + _dot_nt(et, _stack_heads(ybp[q] - hi[q])) for q in qs]
    ds = [ys[q] - _dot(ys[q].astype(BF16), bd) * inv_n for q in qs]
    var = [_dot((d * d).astype(BF16), bd) * inv_n for d in ds]
    for q in qs:
        bb, p = divmod(q, _N_PAIR)
        yn = ds[q] * lax.rsqrt(var[q] + GN_EPS) * lg_ref[p] + lb_ref[p]
        ob_ref[bb, :, sls[p]] = (yn + bon_s[bb, :, sls[p]]) * g_s[bb, :, sls[p]]

    @pl.when(c == pl.num_programs(1) - 1)
    def _():
        sout_ref[...] = s_s[...].reshape(sout_ref.shape)


_RWKV_NB = 2


def _rwkv(pr, prev_row, s0_pairs, prm, batch, seq, tc):
    n = batch * seq
    nt = seq // tc
    nb = _RWKV_NB
    row = lambda a: a.reshape(1, -1).astype(F32)
    zeros = jnp.zeros((HEAD_SIZE_B, W_B), F32)
    wd = jnp.concatenate([prm['decay_lora_up'], zeros], axis=0).astype(BF16)
    wa = jnp.concatenate([zeros, prm['iclr_lora_up']], axis=0).astype(BF16)
    wg = prm['gate_lora_up'].astype(BF16)
    ones_h = jnp.ones((HEAD_SIZE_B, HEAD_SIZE_B), F32)
    bd = jnp.kron(jnp.eye(2, dtype=F32), ones_h).astype(BF16)
    ones4 = jnp.kron(jnp.eye(4, dtype=F32), ones_h).astype(BF16)
    eye2 = jnp.concatenate([jnp.eye(HEAD_SIZE_B, dtype=F32)] * 2, axis=1).astype(BF16)
    idx = jnp.arange(LANES)
    hd, tt = idx // HEAD_SIZE_B, idx % HEAD_SIZE_B
    same = (hd[:, None] == hd[None, :]) & ((tt[:, None] // _GRP) == (tt[None, :] // _GRP))
    mle = (same & (tt[:, None] <= tt[None, :])).astype(F32)
    mlt = (same & (tt[:, None] < tt[None, :])).astype(F32)
    lg = prm['lnx_g'].reshape(_N_PAIR, 1, LANES)
    lb = prm['lnx_b'].reshape(_N_PAIR, 1, LANES)
    full = lambda shape: pl.BlockSpec(shape, lambda b, c: (0,) * len(shape))
    vec = full((1, W_B))
    sq = full((LANES, LANES))
    state_spec = pl.BlockSpec((nb, _N_PAIR, HEAD_SIZE_B, LANES), lambda b, c: (b, 0, 0, 0))
    pair_f32 = pltpu.VMEM((nb * _N_PAIR, HEAD_SIZE_B, LANES), F32)
    pair_sq = pltpu.VMEM((nb * _N_PAIR, LANES, LANES), BF16)
    o_b, s_new = pl.pallas_call(
        _rwkv_kernel,
        grid=(batch // nb, nt),
        in_specs=[
            pl.BlockSpec((nb, tc, C_SHIFT), lambda b, c: (b, c, 0)),
            pl.BlockSpec((nb, 1, C_SHIFT), lambda b, c: (b, 0, 0)),
            state_spec,
            full((1, C_SHIFT)), vec, full((LANES, W_B)), vec, full((LANES, W_B)), full((LANES, W_B)),
            vec, vec, vec,
            full((_N_PAIR, 1, LANES)), full((_N_PAIR, 1, LANES)),
            sq, full((2 * LANES, 2 * LANES)), full((HEAD_SIZE_B, LANES)), sq, sq,
        ],
        out_specs=[
            pl.BlockSpec((nb, tc, W_B), lambda b, c: (b, c, 0)),
            state_spec,
        ],
        out_shape=[
            jax.ShapeDtypeStruct((batch, seq, W_B), F32),
            jax.ShapeDtypeStruct((batch, _N_PAIR, HEAD_SIZE_B, LANES), F32),
        ],
        scratch_shapes=[
            pair_f32,
            pltpu.VMEM((nb, 1, C_SHIFT), F32),
        ] + [pltpu.VMEM((nb, tc, W_B), F32)] * 5 + [
            pltpu.VMEM((nb * _N_PAIR, HEAD_SIZE_B, LANES), BF16),
            pair_f32, pair_f32, pair_f32,
            pair_sq, pair_sq, pair_sq, pair_sq,
        ],
        compiler_params=_cparams(("parallel", "arbitrary")),
        name="rwkv",
    )(pr.reshape(batch, seq, C_SHIFT), prev_row, s0_pairs, row(prm['shift_mu']), row(prm['decay_w0']), wd, row(prm['iclr_a0']),
      wa, wg, row(prm['k_k']), row(prm['k_a']), row(prm['r_k']), lg, lb, bd, ones4, eye2, mle, mlt)
    return o_b.reshape(n, W_B), s_new


def _merge_kernel(oa_ref, ob_ref, g_ref, x_ref, gt1_ref, sc2_ref, sh2_ref, ng_ref, w_ref, x1_ref, h2_ref):
    g = g_ref[...].astype(F32)
    mixed = (g[:, :D_MODEL] * oa_ref[...] + g[:, D_MODEL:] * ob_ref[...]).astype(BF16)
    x1 = x_ref[...] + gt1_ref[0] * _dot(mixed, w_ref[...])
    x1_ref[...] = x1
    ms = jnp.mean(x1 * x1, axis=-1, keepdims=True)
    y = x1 * lax.rsqrt(ms + 1e-6) * ng_ref[...]
    h2_ref[...] = (y * (1.0 + sc2_ref[0]) + sh2_ref[0]).astype(BF16)


def _merge(o_a, o_b, gates, x2d, gt1, sc2, sh2, norm_g, w_out_bf, tm, rows_per_mod):
    n = x2d.shape[0]
    r = gt1.shape[1]
    rowblk = pl.BlockSpec((tm, D_MODEL), lambda i: (i, 0))
    mod_spec = pl.BlockSpec((1, r, D_MODEL), lambda i: (i // rows_per_mod, 0, 0))
    return pl.pallas_call(
        _merge_kernel,
        grid=(n // tm,),
        in_specs=[
            rowblk, rowblk,
            pl.BlockSpec((tm, 2 * D_MODEL), lambda i: (i, 0)),
            rowblk, mod_spec, mod_spec, mod_spec,
            pl.BlockSpec((1, D_MODEL), lambda i: (0, 0)),
            pl.BlockSpec((D_MODEL, D_MODEL), lambda i: (0, 0)),
        ],
        out_specs=[rowblk, rowblk],
        out_shape=[jax.ShapeDtypeStruct((n, D_MODEL), F32), jax.ShapeDtypeStruct((n, D_MODEL), BF16)],
        compiler_params=_cparams(("parallel",)),
        name="merge",
    )(o_a, o_b, gates, x2d, gt1, sc2, sh2, norm_g.reshape(1, -1), w_out_bf)


_NTOP = PEER_TOPK + 1
_CAND = [(a, b) for a in range(_NTOP) for b in range(_NTOP) if (a + 1) * (b + 1) <= _NTOP]


def _peer_prep_kernel(h2_ref, wq_ref, keys_ref, s2_ref, e2_ref, th_ref, cc_ref, s1_s, s2_s, v1_s, v2_s):
    h2 = h2_ref[...]
    ninf = -jnp.inf

    hpi = 2

    def heads(it, carry):
        chains = []
        for hh in range(hpi):
            h = it * hpi + hh
            qh = _dot(h2, wq_ref[h]).astype(BF16)
            for c, (s_s, v_s) in enumerate(((s1_s, v1_s), (s2_s, v2_s))):
                s = _dot_nt(keys_ref[2 * h + c], qh[:, c * N_KEYS:(c + 1) * N_KEYS])
                s_s[h] = s
                chains.append([h, v_s, s])
        for lg in range(h2.shape[0] // LANES):
            ln = slice(lg * LANES, (lg + 1) * LANES)
            cur = [s[:, ln] for (_, _, s) in chains]
            for a in range(_NTOP):
                for ci, (h, v_s, _) in enumerate(chains):
                    m = jnp.max(cur[ci], axis=0, keepdims=True)
                    v_s[h, a, :, ln] = m
                    cur[ci] = jnp.where(cur[ci] == m, ninf, cur[ci])
        return carry

    lax.fori_loop(0, PEER_HEADS // hpi, heads, 0)

    v1 = [jnp.concatenate([v1_s[h, a] for h in range(PEER_HEADS)], axis=0) for a in range(_NTOP)]
    v2 = [jnp.concatenate([v2_s[h, a] for h in range(PEER_HEADS)], axis=0) for a in range(_NTOP)]
    cand = [v1[a] + v2[b] for (a, b) in _CAND]
    tops = []
    for _ in range(_NTOP):
        m = cand[0]
        for cv in cand[1:]:
            m = jnp.maximum(m, cv)
        tops.append(m)
        rem = jnp.ones_like(m)
        nxt = []
        for cv in cand:
            hit = jnp.where(cv == m, rem, 0.0)
            nxt.append(jnp.where(hit > 0.0, ninf, cv))
            rem = rem - hit
        cand = nxt
    tau = jnp.where(tops[PEER_TOPK] == ninf, tops[PEER_TOPK - 1], 0.5 * (tops[PEER_TOPK - 1] + tops[PEER_TOPK]))
    z = jnp.exp(tops[0] - tops[0])
    for a in range(1, PEER_TOPK):
        z = z + jnp.exp(tops[a] - tops[0])
    rz = 1.0 / z
    for h in range(PEER_HEADS):
        s1 = s1_s[h]
        s2 = s2_s[h]
        s2_ref[h] = s2
        e2_ref[h] = jnp.exp(s2 - v2[0][h:h + 1]).astype(BF16)
        th_ref[h] = tau[h:h + 1] - s1
        cc_ref[h] = (jnp.exp(s1 - v1[0][h:h + 1]) * rz[h:h + 1]).astype(BF16)


def _peer_prep(h2, wq_heads, keys_bf, tb):
    n = h2.shape[0]
    out = jax.ShapeDtypeStruct((PEER_HEADS, N_KEYS, n), F32)
    out_bf = jax.ShapeDtypeStruct((PEER_HEADS, N_KEYS, n), BF16)
    ospec = pl.BlockSpec((PEER_HEADS, N_KEYS, tb), lambda i: (0, 0, i))
    return pl.pallas_call(
        _peer_prep_kernel,
        grid=(n // tb,),
        in_specs=[
            pl.BlockSpec((tb, D_MODEL), lambda i: (i, 0)),
            pl.BlockSpec((PEER_HEADS, D_MODEL, 2 * N_KEYS), lambda i: (0, 0, 0)),
            pl.BlockSpec((2 * PEER_HEADS, N_KEYS, N_KEYS), lambda i: (0, 0, 0)),
        ],
        out_specs=[ospec] * 4,
        out_shape=[out, out_bf, out, out_bf],
        scratch_shapes=[
            pltpu.VMEM((PEER_HEADS, N_KEYS, tb), F32), pltpu.VMEM((PEER_HEADS, N_KEYS, tb), F32),
            pltpu.VMEM((PEER_HEADS, _NTOP, 1, tb), F32), pltpu.VMEM((PEER_HEADS, _NTOP, 1, tb), F32),
        ],
        compiler_params=_cparams(("parallel",)),
        name="peer_prep",
    )(h2, wq_heads, keys_bf)


_TE = 1024
_ROWS = _TE // N_KEYS


def _peer_dense_kernel(h2_ref, u_ref, vt_ref, s2_ref, e2_ref, th_ref, cc_ref, x1_ref, gt2_ref, o_ref, acc, p_s):
    e = pl.program_id(1)

    @pl.when(e == 0)
    def _():
        acc[...] = jnp.zeros_like(acc)

    base = pl.multiple_of(e * _ROWS, _ROWS)
    th8 = [th_ref[h, pl.ds(base, _ROWS), :] for h in range(PEER_HEADS)]
    cc8 = [cc_ref[h, pl.ds(base, _ROWS), :] for h in range(PEER_HEADS)]
    ht = _dot_nt(u_ref[...], h2_ref[...])
    for r in range(_ROWS):
        rs = slice(r * N_KEYS, (r + 1) * N_KEYS)
        g = None
        for h in range(PEER_HEADS):
            w = jnp.where(s2_ref[h] >= th8[h][r:r + 1], e2_ref[h] * cc8[h][r:r + 1], jnp.zeros((), BF16))
            g = w if g is None else g + w
        x = ht[rs]
        gelu = 0.5 * x * (1.0 + lax.erf(x * (2.0 ** -0.5)))
        p_s[rs, :] = g * gelu.astype(BF16)
    acc[...] += _dot(vt_ref[...], p_s[...])

    @pl.when(e == pl.num_programs(1) - 1)
    def _():
        o_ref[...] = x1_ref[...] + gt2_ref[0] * acc[...].T


def _peer_dense(h2, u_bf, vt_bf, s2, e2, th, cc, x1, gt2, tb, rows_per_mod):
    n = h2.shape[0]
    ne = u_bf.shape[0] // _TE
    r = gt2.shape[1]
    tok = pl.BlockSpec((PEER_HEADS, N_KEYS, tb), lambda i, e: (0, 0, i))
    rowblk = pl.BlockSpec((tb, D_MODEL), lambda i, e: (i, 0))
    return pl.pallas_call(
        _peer_dense_kernel,
        grid=(n // tb, ne),
        in_specs=[
            rowblk,
            pl.BlockSpec((_TE, D_MODEL), lambda i, e: (e, 0)),
            pl.BlockSpec((D_MODEL, _TE), lambda i, e: (0, e)),
            tok, tok, tok, tok,
            rowblk,
            pl.BlockSpec((1, r, D_MODEL), lambda i, e: (i // rows_per_mod, 0, 0)),
        ],
        out_specs=rowblk,
        out_shape=jax.ShapeDtypeStruct((n, D_MODEL), F32),
        scratch_shapes=[pltpu.VMEM((D_MODEL, tb), F32), pltpu.VMEM((_TE, tb), BF16)],
        compiler_params=_cparams(("parallel", "arbitrary")),
        name="peer_dense",
    )(h2, u_bf, vt_bf, s2, e2, th, cc, x1, gt2)


def _state_to_pairs(s):
    b = s.shape[0]
    return s.reshape(b, _N_PAIR, 2, HEAD_SIZE_B, HEAD_SIZE_B).transpose(0, 1, 3, 2, 4).reshape(b, _N_PAIR, HEAD_SIZE_B, LANES)


def _pairs_to_state(s):
    b = s.shape[0]
    return s.reshape(b, _N_PAIR, HEAD_SIZE_B, 2, HEAD_SIZE_B).transpose(0, 1, 3, 2, 4).reshape(b, N_HEADS_B, HEAD_SIZE_B, HEAD_SIZE_B)


def _stream(x, mods, w, prm, past, cfg):
    batch, t, _ = x.shape
    n = batch * t
    tm, rpm, tc, tb, rpm_tb = cfg
    sh1, sc1, gt1, sh2, sc2, gt2 = mods
    x2d = x.reshape(n, D_MODEL)
    q, k, v, pr, gates = _in_proj(x2d, sc1, sh1, prm['norm_mix_g'], w['w_in'], prm['q_norm_g'], prm['k_norm_g'], tm, rpm)
    if past is None:
        o_a = _attn_prompt(q, k, v, w['slopes'], prm['lambda_qk'], prm['subln_g'], batch, t, w['lam_init'])
        prev = jnp.zeros((batch, 1, C_SHIFT), F32)
        s0 = jnp.zeros((batch, _N_PAIR, HEAD_SIZE_B, LANES), F32)
    else:
        k_past, v_past, s_past, prev_row = past
        o_a = _attn_sample(q, k, v, k_past, v_past, w['slopes'],
                           prm['lambda_qk'], prm['subln_g'], batch, t, w['lam_init'])
        prev = prev_row.reshape(batch, 1, C_SHIFT)
        s0 = _state_to_pairs(s_past)
    o_b, s_pairs = _rwkv(pr, prev, s0, prm, batch, t, tc)
    x1, h2 = _merge(o_a, o_b, gates, x2d, gt1, sc2, sh2, prm['norm_ffn_g'], w['w_out'], tm if tm <= 512 else 512,
                    rpm if tm <= 512 else rpm * (tm // 512))
    s2, e2, th, cc = _peer_prep(h2, w['wq'], w['keys'], tb)
    y = _peer_dense(h2, w['u'], w['vt'], s2, e2, th, cc, x1, gt2, tb, rpm_tb)
    new_k = k.reshape(batch, t, N_HEADS_A, 2 * HEAD_DIM_A)
    new_v = v.reshape(batch, t, N_HEADS_A, 2 * HEAD_DIM_A)
    shift = pr.reshape(batch, t, C_SHIFT)[:, -1]
    return y.reshape(batch, t, D_MODEL), (new_k, new_v, _pairs_to_state(s_pairs), shift)


def kernel(x_prompt, x_sample, c_prompt, c_sample, cache_attn_k, cache_attn_v, state_rwkv_wkv, state_rwkv_shift, norm_mix_g, norm_ffn_g, w_ada, b_ada, w_in, q_norm_g, k_norm_g, lambda_qk, subln_g, shift_mu, decay_w0, decay_lora_up, iclr_a0, iclr_lora_up, gate_lora_up, k_k, k_a, r_k, lnx_g, lnx_b, w_out, peer_w_query, peer_sub_keys, peer_u, peer_v):
    depth = w_in.shape[0]
    bp, tp, _ = x_prompt.shape
    bs, ts, _ = x_sample.shape
    slopes = jnp.asarray([2.0 ** (-8.0 * (h + 1) / N_HEADS_A) for h in range(N_HEADS_A)], F32)
    st_p, st_s = [], []
    for l in range(depth):
        prm = {
            'norm_mix_g': norm_mix_g[l], 'norm_ffn_g': norm_ffn_g[l], 'q_norm_g': q_norm_g[l], 'k_norm_g': k_norm_g[l],
            'lambda_qk': lambda_qk[l], 'subln_g': subln_g[l], 'shift_mu': shift_mu[l], 'decay_w0': decay_w0[l],
            'decay_lora_up': decay_lora_up[l], 'iclr_a0': iclr_a0[l], 'iclr_lora_up': iclr_lora_up[l],
            'gate_lora_up': gate_lora_up[l], 'k_k': k_k[l], 'k_a': k_a[l], 'r_k': r_k[l], 'lnx_g': lnx_g[l], 'lnx_b': lnx_b[l],
        }
        w = {
            'w_in': w_in[l].astype(BF16),
            'w_out': w_out[l].astype(BF16),
            'wq': peer_w_query[l].reshape(D_MODEL, PEER_HEADS, 2 * N_KEYS).transpose(1, 0, 2).astype(BF16),
            'keys': peer_sub_keys[l].reshape(2 * PEER_HEADS, N_KEYS, N_KEYS).astype(BF16),
            'u': peer_u[l].astype(BF16),
            'vt': peer_v[l].T.astype(BF16),
            'slopes': slopes,
            'lam_init': 0.8 - 0.6 * math.exp(-0.3 * l),
        }
        ada = _ada(jnp.concatenate([c_prompt, c_sample], axis=0), w_ada[l], b_ada[l])
        mods = jnp.split(ada, 6, axis=-1)
        mods_p = [m[:bp].reshape(bp, 1, D_MODEL) for m in mods]
        mods_s = [jnp.repeat(m[bp:], ts, axis=0).reshape(1, bs * ts, D_MODEL) for m in mods]
        x_prompt, sp = _stream(x_prompt, mods_p, w, prm, None, (2048, tp // 2048, CHUNK, 512, tp // 512))
        x_sample, ss = _stream(x_sample, mods_s, w, prm,
                               (cache_attn_k[l], cache_attn_v[l], state_rwkv_wkv[l], state_rwkv_shift[l]),
                               (bs * ts, 1, ts, bs * ts, 1))
        st_p.append(sp)
        st_s.append(ss)
    stack = lambda sts, i: jnp.stack([s[i] for s in sts])
    return (x_prompt, x_sample, stack(st_p, 0), stack(st_p, 1), stack(st_p, 2), stack(st_p, 3),
            stack(st_s, 0), stack(st_s, 1), stack(st_s, 2), stack(st_s, 3))
```

```python
import functools
import math

import jax
import jax.numpy as jnp
from jax import lax
from jax.experimental import pallas as pl
from jax.experimental.pallas import tpu as pltpu

F32 = jnp.float32
BF16 = jnp.bfloat16

D_MODEL = 1024
N_HEADS_A = 8
HEAD_DIM_A = 64
W_A = 1024
N_HEADS_B = 16
HEAD_SIZE_B = 64
W_B = 1024
C_SHIFT = 3328
D_IN = 8448
CHUNK = 64
GN_EPS = 64e-5
N_KEYS = 128
PEER_HEADS = 8
PEER_TOPK = 16
NEG = -1e30
LANES = 128
VMEM_LIMIT = 56 * 1024 * 1024

_NT = (((1,), (1,)), ((), ()))


def _cparams(sem):
    return pltpu.CompilerParams(dimension_semantics=sem, vmem_limit_bytes=VMEM_LIMIT)


def _dot(a, b):
    return jnp.dot(a, b, preferred_element_type=F32)


def _dot_nt(a, b):
    return lax.dot_general(a, b, _NT, preferred_element_type=F32)


def _ada_kernel(c_ref, w_ref, b_ref, o_ref):
    c = c_ref[...]
    s = c * jax.nn.sigmoid(c)
    o_ref[...] = _dot(s.astype(BF16), w_ref[...].astype(BF16)) + b_ref[...]


def _ada(c, w_ada, b_ada):
    n = c.shape[0]
    tn = 1024
    return pl.pallas_call(
        _ada_kernel,
        grid=(6 * D_MODEL // tn,),
        in_specs=[
            pl.BlockSpec((n, D_MODEL), lambda j: (0, 0)),
            pl.BlockSpec((D_MODEL, tn), lambda j: (0, j)),
            pl.BlockSpec((1, tn), lambda j: (0, j)),
        ],
        out_specs=pl.BlockSpec((n, tn), lambda j: (0, j)),
        out_shape=jax.ShapeDtypeStruct((n, 6 * D_MODEL), F32),
        compiler_params=_cparams(("arbitrary",)),
        name="ada",
    )(c, w_ada, b_ada.reshape(1, -1))


_IN_TN = 256
_Q_T, _K_T, _V_T, _PR_T, _G_T = 0, 4, 8, 12, 25
_N_T = D_IN // _IN_TN


def _in_kernel(x_ref, sc_ref, sh_ref, g_ref, w_ref, qg_ref, kg_ref, bd_ref,
               q_ref, k_ref, v_ref, pr_ref, gt_ref, h_scr):
    j = pl.program_id(1)

    @pl.when(j == 0)
    def _():
        x = x_ref[...]
        ms = jnp.mean(x * x, axis=-1, keepdims=True)
        y = x * lax.rsqrt(ms + 1e-6) * g_ref[...]
        h_scr[...] = (y * (1.0 + sc_ref[0]) + sh_ref[0]).astype(BF16)

    acc = _dot(h_scr[...], w_ref[...])

    def head_norm(a, g):
        ms = _dot((a * a).astype(BF16), bd_ref[...])
        return a * lax.rsqrt(ms + 1e-6) * g

    @pl.when(j < _K_T)
    def _():
        q_ref[...] = head_norm(acc, qg_ref[...]).astype(BF16)

    @pl.when((j >= _K_T) & (j < _V_T))
    def _():
        k_ref[...] = head_norm(acc, kg_ref[...])

    @pl.when((j >= _V_T) & (j < _PR_T))
    def _():
        v_ref[...] = acc

    @pl.when((j >= _PR_T) & (j < _G_T))
    def _():
        pr_ref[...] = acc

    @pl.when(j >= _G_T)
    def _():
        gt_ref[...] = jax.nn.sigmoid(acc).astype(BF16)


def _in_proj(x2d, sc, sh, norm_g, w_in_bf, q_g, k_g, tm, rows_per_mod):
    n = x2d.shape[0]
    tn = _IN_TN
    r = sc.shape[1]
    bd = jnp.kron(jnp.eye(tn // HEAD_DIM_A, dtype=F32), jnp.full((HEAD_DIM_A, HEAD_DIM_A), 1.0 / HEAD_DIM_A, F32)).astype(BF16)
    qg = jnp.tile(q_g.reshape(1, -1), (1, tn // HEAD_DIM_A))
    kg = jnp.tile(k_g.reshape(1, -1), (1, tn // HEAD_DIM_A))

    def seg(first, count):
        return lambda i, j: (i, jnp.clip(j - first, 0, count - 1))

    mod_spec = pl.BlockSpec((1, r, D_MODEL), lambda i, j: (i // rows_per_mod, 0, 0))
    return pl.pallas_call(
        _in_kernel,
        grid=(n // tm, _N_T),
        in_specs=[
            pl.BlockSpec((tm, D_MODEL), lambda i, j: (i, 0)),
            mod_spec, mod_spec,
            pl.BlockSpec((1, D_MODEL), lambda i, j: (0, 0)),
            pl.BlockSpec((D_MODEL, tn), lambda i, j: (0, j)),
            pl.BlockSpec((1, tn), lambda i, j: (0, 0)),
            pl.BlockSpec((1, tn), lambda i, j: (0, 0)),
            pl.BlockSpec((tn, tn), lambda i, j: (0, 0)),
        ],
        out_specs=[
            pl.BlockSpec((tm, tn), seg(_Q_T, 4)),
            pl.BlockSpec((tm, tn), seg(_K_T, 4)),
            pl.BlockSpec((tm, tn), seg(_V_T, 4)),
            pl.BlockSpec((tm, tn), seg(_PR_T, 13)),
            pl.BlockSpec((tm, tn), seg(_G_T, 8)),
        ],
        out_shape=[
            jax.ShapeDtypeStruct((n, W_A), BF16),
            jax.ShapeDtypeStruct((n, W_A), F32),
            jax.ShapeDtypeStruct((n, W_A), F32),
            jax.ShapeDtypeStruct((n, C_SHIFT), F32),
            jax.ShapeDtypeStruct((n, 2 * D_MODEL), BF16),
        ],
        scratch_shapes=[pltpu.VMEM((tm, D_MODEL), BF16)],
        compiler_params=_cparams(("parallel", "arbitrary")),
        name="in_proj",
    )(x2d, sc, sh, norm_g.reshape(1, -1), w_in_bf, qg, kg, bd)


_TQ = 512


def _lam(lq_ref, lam_init):
    lq = lq_ref[...]
    a = jnp.sum(lq[0:1] * lq[1:2], axis=-1, keepdims=True)
    b = jnp.sum(lq[2:3] * lq[3:4], axis=-1, keepdims=True)
    return jnp.exp(a) - jnp.exp(b) + lam_init


def _subln(o, sg_ref, lam_init):
    ms = jnp.mean(o * o, axis=-1, keepdims=True)
    return o * lax.rsqrt(ms + 1e-6) * sg_ref[...] * (1.0 - lam_init)


def _split_q(q):
    lane = lax.broadcasted_iota(jnp.int32, q.shape, 1)
    qs = q * jnp.asarray(HEAD_DIM_A ** -0.5, BF16)
    zero = jnp.zeros_like(qs)
    return jnp.where(lane < HEAD_DIM_A, qs, zero), jnp.where(lane >= HEAD_DIM_A, qs, zero)


def _attn_prompt_kernel(slopes_ref, q_ref, k_ref, v_ref, lq_ref, sg_ref, o_ref, kb, vb, *, lam_init):
    h = pl.program_id(1)
    i = pl.program_id(2)
    tq = _TQ

    @pl.when(i == 0)
    def _():
        kb[...] = k_ref[...].astype(BF16)
        vb[:, :LANES] = v_ref[...].astype(BF16)
        vb[:, LANES:] = jnp.ones((vb.shape[0], LANES), BF16)

    slope = slopes_ref[h]
    qc = _split_q(q_ref[...])
    iq = lax.broadcasted_iota(jnp.int32, (tq, tq), 0)
    ik = lax.broadcasted_iota(jnp.int32, (tq, tq), 1)
    rel = (ik - iq).astype(F32)
    bias_off = slope * rel
    visible = (ik // CHUNK) <= (iq // CHUNK)
    bias_diag = jnp.where(visible, -slope * jnp.abs(rel), NEG)

    def block(j, bias, cst, carry):
        kj = kb[pl.ds(pl.multiple_of(j * tq, tq), tq), :]
        vj = vb[pl.ds(pl.multiple_of(j * tq, tq), tq), :]
        s = [_dot_nt(qc[c], kj) + bias for c in range(2)]
        m_new = [jnp.maximum(carry[c][0], jnp.max(s[c], axis=-1, keepdims=True) + cst) for c in range(2)]
        p = [jnp.exp(s[c] - (m_new[c] - cst)).astype(BF16) for c in range(2)]
        pv = [_dot(p[c], vj) for c in range(2)]
        return tuple((m_new[c], carry[c][1] * jnp.exp(carry[c][0] - m_new[c]) + pv[c]) for c in range(2))

    init = tuple((jnp.full((tq, 1), NEG, F32), jnp.zeros((tq, 2 * LANES), F32)) for _ in range(2))

    def off_body(j, carry):
        cst = -slope * ((i - j) * tq).astype(F32)
        return block(j, bias_off, cst, carry)

    carry = lax.fori_loop(0, i, off_body, init)
    carry = block(i, bias_diag, jnp.zeros((), F32), carry)

    lam = _lam(lq_ref, lam_init)
    o = [acc[:, :LANES] / acc[:, LANES:LANES + 1] for (_, acc) in carry]
    o_ref[...] = _subln(o[0] - lam * o[1], sg_ref, lam_init)


def _attn_prompt(q, k, v, slopes, lambda_qk, subln_g, batch, seq, lam_init):
    n = batch * seq
    nq = seq // _TQ
    kern = functools.partial(_attn_prompt_kernel, lam_init=lam_init)
    return pl.pallas_call(
        kern,
        grid=(batch, N_HEADS_A, nq),
        in_specs=[
            pl.BlockSpec(memory_space=pltpu.SMEM),
            pl.BlockSpec((_TQ, LANES), lambda b, h, i: (b * nq + i, h)),
            pl.BlockSpec((seq, LANES), lambda b, h, i: (b, h)),
            pl.BlockSpec((seq, LANES), lambda b, h, i: (b, h)),
            pl.BlockSpec((4, HEAD_DIM_A), lambda b, h, i: (0, 0)),
            pl.BlockSpec((1, LANES), lambda b, h, i: (0, 0)),
        ],
        out_specs=pl.BlockSpec((_TQ, LANES), lambda b, h, i: (b * nq + i, h)),
        out_shape=jax.ShapeDtypeStruct((n, W_A), F32),
        scratch_shapes=[pltpu.VMEM((seq, LANES), BF16), pltpu.VMEM((seq, 2 * LANES), BF16)],
        compiler_params=_cparams(("parallel", "parallel", "arbitrary")),
        name="attn_prompt",
    )(slopes, q, k, v, lambda_qk, subln_g.reshape(1, -1))


def _attn_sample_kernel(slopes_ref, q_ref, kp_ref, vp_ref, kn_ref, vn_ref, lq_ref, sg_ref, o_ref, *, lam_init, past):
    t = q_ref.shape[0]
    lam = _lam(lq_ref, lam_init)

    def bias(nk, k0, slope):
        qpos = past + lax.broadcasted_iota(jnp.int32, (t, nk), 0)
        kpos = k0 + lax.broadcasted_iota(jnp.int32, (t, nk), 1)
        vis = (kpos // CHUNK) <= (qpos // CHUNK)
        return jnp.where(vis, -slope * jnp.abs(qpos - kpos).astype(F32), NEG)

    for h in range(N_HEADS_A):
        sl = slice(h * LANES, (h + 1) * LANES)
        slope = slopes_ref[h]
        qc = _split_q(q_ref[:, sl])
        kp = kp_ref[0, :, h, :].astype(BF16)
        vp = vp_ref[0, :, h, :].astype(BF16)
        kn = kn_ref[:, sl].astype(BF16)
        vn = vn_ref[:, sl].astype(BF16)
        bp = bias(past, 0, slope)
        bn = bias(t, past, slope)
        o = []
        for c in range(2):
            sp = _dot_nt(qc[c], kp) + bp
            sn = _dot_nt(qc[c], kn) + bn
            m = jnp.maximum(jnp.max(sp, axis=-1, keepdims=True), jnp.max(sn, axis=-1, keepdims=True))
            pp = jnp.exp(sp - m)
            pn = jnp.exp(sn - m)
            l = jnp.sum(pp, axis=-1, keepdims=True) + jnp.sum(pn, axis=-1, keepdims=True)
            o.append((_dot(pp.astype(BF16), vp) + _dot(pn.astype(BF16), vn)) / l)
        o_ref[:, sl] = _subln(o[0] - lam * o[1], sg_ref, lam_init)


def _attn_sample(q, k_new, v_new, k_past, v_past, slopes, lambda_qk, subln_g, batch, t, lam_init):
    past = k_past.shape[1]
    kern = functools.partial(_attn_sample_kernel, lam_init=lam_init, past=past)
    tok = pl.BlockSpec((t, W_A), lambda b: (b, 0))
    cache = pl.BlockSpec((1, past, N_HEADS_A, LANES), lambda b: (b, 0, 0, 0))
    return pl.pallas_call(
        kern,
        grid=(batch,),
        in_specs=[
            pl.BlockSpec(memory_space=pltpu.SMEM),
            tok, cache, cache, tok, tok,
            pl.BlockSpec((4, HEAD_DIM_A), lambda b: (0, 0)),
            pl.BlockSpec((1, LANES), lambda b: (0, 0)),
        ],
        out_specs=tok,
        out_shape=jax.ShapeDtypeStruct((batch * t, W_A), F32),
        compiler_params=_cparams(("parallel",)),
        name="attn_sample",
    )(slopes, q, k_past, v_past, k_new, v_new, lambda_qk, subln_g.reshape(1, -1))


_N_PAIR = N_HEADS_B // 2
_XW_COL = 3 * W_B
_XG_COL = 3 * W_B + LANES
_GRP = 16


def _seg_cumprod(x, reverse=False):
    n = x.shape[0]
    ri = lax.broadcasted_iota(jnp.int32, x.shape, 0) % _GRP
    sh = 1
    while sh < _GRP:
        if reverse:
            x = x * jnp.where(ri + sh < _GRP, pltpu.roll(x, n - sh, 0), 1.0)
        else:
            x = x * jnp.where(ri >= sh, pltpu.roll(x, sh, 0), 1.0)
        sh *= 2
    return x


def _stack_heads(x):
    pad = HEAD_SIZE_B - x.shape[0]
    lane = lax.broadcasted_iota(jnp.int32, x.shape, 1)
    parts = []
    for hh in range(2):
        parts.append(jnp.where((lane >= HEAD_SIZE_B) == (hh == 1), x, 0.0))
        if pad:
            parts.append(jnp.zeros((pad, LANES), F32))
    return jnp.concatenate(parts, axis=0).astype(BF16)


def _rwkv_kernel(pr_ref, prev_ref, s0_ref, mu_ref, w0_ref, wd_ref, a0_ref, wa_ref, wg_ref, kk_ref, ka_ref, rk_ref,
                 lg_ref, lb_ref, bd_ref, o4_ref, eye_ref, mle_ref, mlt_ref,
                 ob_ref, sout_ref,
                 s_s, prev_s, w_s, kk_s, b_s, g_s, bon_s, vt_s, ps_s, sab_s, y0_s, r2_s, kd2_s, wb_s, wk_s):
    c = pl.program_id(1)
    nb, tc, _ = pr_ref.shape
    n_pair = nb * _N_PAIR

    @pl.when(c == 0)
    def _():
        prev_s[...] = prev_ref[...]
        s_s[...] = s0_ref[...].reshape(s_s.shape)

    bd = bd_ref[...]
    mle = mle_ref[...]
    mlt = mlt_ref[...]
    for bb in range(nb):
        pr = pr_ref[bb]
        row = lax.broadcasted_iota(jnp.int32, pr.shape, 0)
        shifted = jnp.where(row == 0, prev_s[bb], pltpu.roll(pr, 1, 0))
        prev_s[bb] = pr[tc - 1:tc, :]
        xs = pr + (shifted - pr) * mu_ref[...]
        r = xs[:, 0:W_B]
        k = xs[:, W_B:2 * W_B]
        v = xs[:, 2 * W_B:3 * W_B]
        xwa = xs[:, _XW_COL:_XW_COL + LANES]
        xg = xs[:, _XG_COL:_XG_COL + LANES]
        z = w0_ref[...] + _dot(jnp.tanh(xwa).astype(BF16), wd_ref[...])
        wlog = -(jnp.maximum(-z, 0.0) + jnp.log(1.0 + jnp.exp(-jnp.abs(z)))) - 0.5
        decay = jnp.exp(-jnp.exp(wlog))
        a = jax.nn.sigmoid(a0_ref[...] + _dot(xwa.astype(BF16), wa_ref[...]))
        g_s[bb] = _dot(jax.nn.sigmoid(xg).astype(BF16), wg_ref[...])
        kkraw = k * kk_ref[...]
        k_mod = k * (1.0 + (a - 1.0) * ka_ref[...])
        rkr = r * k_mod * rk_ref[...]
        cum = _seg_cumprod(decay)
        tail = _seg_cumprod(decay, reverse=True) / decay
        rc = r * cum
        ki = k_mod / cum
        kd = k_mod * tail
        w_s[bb] = decay
        pairs = range(_N_PAIR)
        cols = [slice(p * LANES, (p + 1) * LANES) for p in pairs]
        ss = [_dot((kkraw[:, sl] * kkraw[:, sl]).astype(BF16), bd) for sl in cols]
        bons = [_dot(rkr[:, sl].astype(BF16), bd) for sl in cols]
        vts = [_dot_nt(eye_ref[...], _stack_heads(v[:, sl])).astype(BF16) for sl in cols]
        kkn = [kkraw[:, sl] / jnp.maximum(jnp.sqrt(s), 1e-12) for sl, s in zip(cols, ss)]
        bq = [kn * a[:, sl] for sl, kn in zip(cols, kkn)]
        r2 = [_stack_heads(rc[:, sl]) for sl in cols]
        k2 = [_stack_heads(ki[:, sl]) for sl in cols]
        gb = [_dot_nt(_stack_heads(bq[p] / cum[:, cols[p]]), r2[p]) for p in pairs]
        gk = [_dot_nt(k2[p], r2[p]) for p in pairs]
        gg = [_dot_nt(k2[p], _stack_heads(kkn[p] * cum[:, cols[p]] / decay[:, cols[p]])) for p in pairs]
        pss = [_dot(vts[p], (gg[p] * mlt).astype(BF16)) for p in pairs]
        for p in pairs:
            sl = cols[p]
            q = bb * _N_PAIR + p
            kk_s[bb, :, sl] = kkn[p]
            b_s[bb, :, sl] = bq[p]
            bon_s[bb, :, sl] = bons[p] * v[:, sl]
            vt_s[q] = vts[p]
            r2_s[q] = r2[p]
            kd2_s[q] = _stack_heads(kd[:, sl])
            wb_s[q] = (-(gb[p] * mle)).astype(BF16)
            wk_s[q] = (gk[p] * mle).astype(BF16)
            ps_s[q] = pss[p]
    sab_s[...] = jnp.zeros_like(sab_s)
    y0_s[...] = jnp.zeros_like(y0_s)

    ones4 = o4_ref[...]
    lane_t = lax.broadcasted_iota(jnp.int32, (HEAD_SIZE_B, LANES), 1) % HEAD_SIZE_B
    row_g = (lax.broadcasted_iota(jnp.int32, (LANES, LANES), 0) % HEAD_SIZE_B) // _GRP
    sls = [slice(p * LANES, (p + 1) * LANES) for p in range(_N_PAIR)]
    n_half = _N_PAIR // 2
    zero_bf = jnp.zeros((), BF16)

    def stack(mats):
        return jnp.concatenate([jnp.concatenate([mats[2 * q], mats[2 * q + 1]], axis=1) for q in range(len(mats) // 2)], axis=0)

    def unstack(m, j):
        q, o = divmod(j, 2)
        return m[q * HEAD_SIZE_B:(q + 1) * HEAD_SIZE_B, o * LANES:(o + 1) * LANES]

    def group(gi, carry):
        base = pl.multiple_of(gi * _GRP, _GRP)
        rows = lambda ref: [ref[bb, pl.ds(base, _GRP), :] for bb in range(nb)]
        kk8, w8, b8 = rows(kk_s), rows(w_s), rows(b_s)
        row = lambda tiles, p, i: tiles[p // _N_PAIR][i:i + 1, sls[p % _N_PAIR]]
        in_group = row_g == gi
        st = [s_s[p] for p in range(n_pair)]
        for p in range(n_pair):
            y0_s[p] += _dot_nt(st[p].astype(BF16), jnp.where(in_group, r2_s[p], zero_bf))
        for i in range(_GRP):
            tmask = lane_t == (base + i)
            for half in range(2 * nb):
                ps = list(range(half * n_half, (half + 1) * n_half))
                lhs = [(st[p] * row(kk8, p, i) + jnp.where(tmask, ps_s[p], 0.0)).astype(BF16) for p in ps]
                sa = _dot(stack(lhs), ones4)
                for j, p in enumerate(ps):
                    sa_p = unstack(sa, j)
                    sab_s[p] = jnp.where(tmask, sa_p, sab_s[p])
                    st[p] = st[p] * row(w8, p, i) - sa_p * row(b8, p, i)
        for p in range(n_pair):
            s_s[p] = st[p] + _dot(vt_s[p], jnp.where(in_group, kd2_s[p], zero_bf))
        return carry

    lax.fori_loop(0, tc // _GRP, group, 0)

    et = eye_ref[:tc]
    inv_n = 1.0 / HEAD_SIZE_B
    qs = range(n_pair)
    ybp = [y0_s[q] + _dot(sab_s[q].astype(BF16), wb_s[q]) + _dot(vt_s[q], wk_s[q]) for q in qs]
    hi = [x.astype(BF16).astype(F32) for x in ybp]
    ys = [_dot_nt(et, _stack_heads(hi[q])) + _dot_nt(et, _stack_heads(ybp[q] - hi[q])) for q in qs]
    ds = [ys[q] - _dot(ys[q].astype(BF16), bd) * inv_n for q in qs]
    var = [_dot((d * d).astype(BF16), bd) * inv_n for d in ds]
    for q in qs:
        bb, p = divmod(q, _N_PAIR)
        yn = ds[q] * lax.rsqrt(var[q] + GN_EPS) * lg_ref[p] + lb_ref[p]
        ob_ref[bb, :, sls[p]] = (yn + bon_s[bb, :, sls[p]]) * g_s[bb, :, sls[p]]

    @pl.when(c == pl.num_programs(1) - 1)
    def _():
        sout_ref[...] = s_s[...].reshape(sout_ref.shape)


_RWKV_NB = 2


def _rwkv(pr, prev_row, s0_pairs, prm, batch, seq, tc):
    n = batch * seq
    nt = seq // tc
    nb = _RWKV_NB
    row = lambda a: a.reshape(1, -1).astype(F32)
    zeros = jnp.zeros((HEAD_SIZE_B, W_B), F32)
    wd = jnp.concatenate([prm['decay_lora_up'], zeros], axis=0).astype(BF16)
    wa = jnp.concatenate([zeros, prm['iclr_lora_up']], axis=0).astype(BF16)
    wg = prm['gate_lora_up'].astype(BF16)
    ones_h = jnp.ones((HEAD_SIZE_B, HEAD_SIZE_B), F32)
    bd = jnp.kron(jnp.eye(2, dtype=F32), ones_h).astype(BF16)
    ones4 = jnp.kron(jnp.eye(4, dtype=F32), ones_h).astype(BF16)
    eye2 = jnp.concatenate([jnp.eye(HEAD_SIZE_B, dtype=F32)] * 2, axis=1).astype(BF16)
    idx = jnp.arange(LANES)
    hd, tt = idx // HEAD_SIZE_B, idx % HEAD_SIZE_B
    same = (hd[:, None] == hd[None, :]) & ((tt[:, None] // _GRP) == (tt[None, :] // _GRP))
    mle = (same & (tt[:, None] <= tt[None, :])).astype(F32)
    mlt = (same & (tt[:, None] < tt[None, :])).astype(F32)
    lg = prm['lnx_g'].reshape(_N_PAIR, 1, LANES)
    lb = prm['lnx_b'].reshape(_N_PAIR, 1, LANES)
    full = lambda shape: pl.BlockSpec(shape, lambda b, c: (0,) * len(shape))
    vec = full((1, W_B))
    sq = full((LANES, LANES))
    state_spec = pl.BlockSpec((nb, _N_PAIR, HEAD_SIZE_B, LANES), lambda b, c: (b, 0, 0, 0))
    pair_f32 = pltpu.VMEM((nb * _N_PAIR, HEAD_SIZE_B, LANES), F32)
    pair_sq = pltpu.VMEM((nb * _N_PAIR, LANES, LANES), BF16)
    o_b, s_new = pl.pallas_call(
        _rwkv_kernel,
        grid=(batch // nb, nt),
        in_specs=[
            pl.BlockSpec((nb, tc, C_SHIFT), lambda b, c: (b, c, 0)),
            pl.BlockSpec((nb, 1, C_SHIFT), lambda b, c: (b, 0, 0)),
            state_spec,
            full((1, C_SHIFT)), vec, full((LANES, W_B)), vec, full((LANES, W_B)), full((LANES, W_B)),
            vec, vec, vec,
            full((_N_PAIR, 1, LANES)), full((_N_PAIR, 1, LANES)),
            sq, full((2 * LANES, 2 * LANES)), full((HEAD_SIZE_B, LANES)), sq, sq,
        ],
        out_specs=[
            pl.BlockSpec((nb, tc, W_B), lambda b, c: (b, c, 0)),
            state_spec,
        ],
        out_shape=[
            jax.ShapeDtypeStruct((batch, seq, W_B), F32),
            jax.ShapeDtypeStruct((batch, _N_PAIR, HEAD_SIZE_B, LANES), F32),
        ],
        scratch_shapes=[
            pair_f32,
            pltpu.VMEM((nb, 1, C_SHIFT), F32),
        ] + [pltpu.VMEM((nb, tc, W_B), F32)] * 5 + [
            pltpu.VMEM((nb * _N_PAIR, HEAD_SIZE_B, LANES), BF16),
            pair_f32, pair_f32, pair_f32,
            pair_sq, pair_sq, pair_sq, pair_sq,
        ],
        compiler_params=_cparams(("parallel", "arbitrary")),
        name="rwkv",
    )(pr.reshape(batch, seq, C_SHIFT), prev_row, s0_pairs, row(prm['shift_mu']), row(prm['decay_w0']), wd, row(prm['iclr_a0']),
      wa, wg, row(prm['k_k']), row(prm['k_a']), row(prm['r_k']), lg, lb, bd, ones4, eye2, mle, mlt)
    return o_b.reshape(n, W_B), s_new


def _merge_kernel(oa_ref, ob_ref, g_ref, x_ref, gt1_ref, sc2_ref, sh2_ref, ng_ref, w_ref, x1_ref, h2_ref):
    g = g_ref[...].astype(F32)
    mixed = (g[:, :D_MODEL] * oa_ref[...] + g[:, D_MODEL:] * ob_ref[...]).astype(BF16)
    x1 = x_ref[...] + gt1_ref[0] * _dot(mixed, w_ref[...])
    x1_ref[...] = x1
    ms = jnp.mean(x1 * x1, axis=-1, keepdims=True)
    y = x1 * lax.rsqrt(ms + 1e-6) * ng_ref[...]
    h2_ref[...] = (y * (1.0 + sc2_ref[0]) + sh2_ref[0]).astype(BF16)


def _merge(o_a, o_b, gates, x2d, gt1, sc2, sh2, norm_g, w_out_bf, tm, rows_per_mod):
    n = x2d.shape[0]
    r = gt1.shape[1]
    rowblk = pl.BlockSpec((tm, D_MODEL), lambda i: (i, 0))
    mod_spec = pl.BlockSpec((1, r, D_MODEL), lambda i: (i // rows_per_mod, 0, 0))
    return pl.pallas_call(
        _merge_kernel,
        grid=(n // tm,),
        in_specs=[
            rowblk, rowblk,
            pl.BlockSpec((tm, 2 * D_MODEL), lambda i: (i, 0)),
            rowblk, mod_spec, mod_spec, mod_spec,
            pl.BlockSpec((1, D_MODEL), lambda i: (0, 0)),
            pl.BlockSpec((D_MODEL, D_MODEL), lambda i: (0, 0)),
        ],
        out_specs=[rowblk, rowblk],
        out_shape=[jax.ShapeDtypeStruct((n, D_MODEL), F32), jax.ShapeDtypeStruct((n, D_MODEL), BF16)],
        compiler_params=_cparams(("parallel",)),
        name="merge",
    )(o_a, o_b, gates, x2d, gt1, sc2, sh2, norm_g.reshape(1, -1), w_out_bf)


_NTOP = PEER_TOPK + 1
_CAND = [(a, b) for a in range(_NTOP) for b in range(_NTOP) if (a + 1) * (b + 1) <= _NTOP]


def _peer_prep_kernel(h2_ref, wq_ref, keys_ref, s2_ref, e2_ref, th_ref, cc_ref, s1_s, s2_s, v1_s, v2_s):
    h2 = h2_ref[...]
    ninf = -jnp.inf

    hpi = 2

    def heads(it, carry):
        chains = []
        for hh in range(hpi):
            h = it * hpi + hh
            qh = _dot(h2, wq_ref[h]).astype(BF16)
            for c, (s_s, v_s) in enumerate(((s1_s, v1_s), (s2_s, v2_s))):
                s = _dot_nt(keys_ref[2 * h + c], qh[:, c * N_KEYS:(c + 1) * N_KEYS])
                s_s[h] = s
                chains.append([h, v_s, s])
        for lg in range(h2.shape[0] // LANES):
            ln = slice(lg * LANES, (lg + 1) * LANES)
            cur = [s[:, ln] for (_, _, s) in chains]
            for a in range(_NTOP):
                for ci, (h, v_s, _) in enumerate(chains):
                    m = jnp.max(cur[ci], axis=0, keepdims=True)
                    v_s[h, a, :, ln] = m
                    cur[ci] = jnp.where(cur[ci] == m, ninf, cur[ci])
        return carry

    lax.fori_loop(0, PEER_HEADS // hpi, heads, 0)

    v1 = [jnp.concatenate([v1_s[h, a] for h in range(PEER_HEADS)], axis=0) for a in range(_NTOP)]
    v2 = [jnp.concatenate([v2_s[h, a] for h in range(PEER_HEADS)], axis=0) for a in range(_NTOP)]
    cand = [v1[a] + v2[b] for (a, b) in _CAND]
    tops = []
    for _ in range(_NTOP):
        m = cand[0]
        for cv in cand[1:]:
            m = jnp.maximum(m, cv)
        tops.append(m)
        rem = jnp.ones_like(m)
        nxt = []
        for cv in cand:
            hit = jnp.where(cv == m, rem, 0.0)
            nxt.append(jnp.where(hit > 0.0, ninf, cv))
            rem = rem - hit
        cand = nxt
    tau = jnp.where(tops[PEER_TOPK] == ninf, tops[PEER_TOPK - 1], 0.5 * (tops[PEER_TOPK - 1] + tops[PEER_TOPK]))
    z = jnp.exp(tops[0] - tops[0])
    for a in range(1, PEER_TOPK):
        z = z + jnp.exp(tops[a] - tops[0])
    rz = 1.0 / z
    for h in range(PEER_HEADS):
        s1 = s1_s[h]
        s2 = s2_s[h]
        s2_ref[h] = s2
        e2_ref[h] = jnp.exp(s2 - v2[0][h:h + 1]).astype(BF16)
        th_ref[h] = tau[h:h + 1] - s1
        cc_ref[h] = (jnp.exp(s1 - v1[0][h:h + 1]) * rz[h:h + 1]).astype(BF16)


def _peer_prep(h2, wq_heads, keys_bf, tb):
    n = h2.shape[0]
    out = jax.ShapeDtypeStruct((PEER_HEADS, N_KEYS, n), F32)
    out_bf = jax.ShapeDtypeStruct((PEER_HEADS, N_KEYS, n), BF16)
    ospec = pl.BlockSpec((PEER_HEADS, N_KEYS, tb), lambda i: (0, 0, i))
    return pl.pallas_call(
        _peer_prep_kernel,
        grid=(n // tb,),
        in_specs=[
            pl.BlockSpec((tb, D_MODEL), lambda i: (i, 0)),
            pl.BlockSpec((PEER_HEADS, D_MODEL, 2 * N_KEYS), lambda i: (0, 0, 0)),
            pl.BlockSpec((2 * PEER_HEADS, N_KEYS, N_KEYS), lambda i: (0, 0, 0)),
        ],
        out_specs=[ospec] * 4,
        out_shape=[out, out_bf, out, out_bf],
        scratch_shapes=[
            pltpu.VMEM((PEER_HEADS, N_KEYS, tb), F32), pltpu.VMEM((PEER_HEADS, N_KEYS, tb), F32),
            pltpu.VMEM((PEER_HEADS, _NTOP, 1, tb), F32), pltpu.VMEM((PEER_HEADS, _NTOP, 1, tb), F32),
        ],
        compiler_params=_cparams(("parallel",)),
        name="peer_prep",
    )(h2, wq_heads, keys_bf)


_TE = 1024
_ROWS = _TE // N_KEYS


def _peer_dense_kernel(h2_ref, u_ref, vt_ref, s2_ref, e2_ref, th_ref, cc_ref, x1_ref, gt2_ref, o_ref, acc, p_s):
    e = pl.program_id(1)

    @pl.when(e == 0)
    def _():
        acc[...] = jnp.zeros_like(acc)

    base = pl.multiple_of(e * _ROWS, _ROWS)
    th8 = [th_ref[h, pl.ds(base, _ROWS), :] for h in range(PEER_HEADS)]
    cc8 = [cc_ref[h, pl.ds(base, _ROWS), :] for h in range(PEER_HEADS)]
    ht = _dot_nt(u_ref[...], h2_ref[...])
    for r in range(_ROWS):
        rs = slice(r * N_KEYS, (r + 1) * N_KEYS)
        g = None
        for h in range(PEER_HEADS):
            w = jnp.where(s2_ref[h] >= th8[h][r:r + 1], e2_ref[h] * cc8[h][r:r + 1], jnp.zeros((), BF16))
            g = w if g is None else g + w
        x = ht[rs]
        gelu = 0.5 * x * (1.0 + lax.erf(x * (2.0 ** -0.5)))
        p_s[rs, :] = g * gelu.astype(BF16)
    acc[...] += _dot(vt_ref[...], p_s[...])

    @pl.when(e == pl.num_programs(1) - 1)
    def _():
        o_ref[...] = x1_ref[...] + gt2_ref[0] * acc[...].T


def _peer_dense(h2, u_bf, vt_bf, s2, e2, th, cc, x1, gt2, tb, rows_per_mod):
    n = h2.shape[0]
    ne = u_bf.shape[0] // _TE
    r = gt2.shape[1]
    tok = pl.BlockSpec((PEER_HEADS, N_KEYS, tb), lambda i, e: (0, 0, i))
    rowblk = pl.BlockSpec((tb, D_MODEL), lambda i, e: (i, 0))
    return pl.pallas_call(
        _peer_dense_kernel,
        grid=(n // tb, ne),
        in_specs=[
            rowblk,
            pl.BlockSpec((_TE, D_MODEL), lambda i, e: (e, 0)),
            pl.BlockSpec((D_MODEL, _TE), lambda i, e: (0, e)),
            tok, tok, tok, tok,
            rowblk,
            pl.BlockSpec((1, r, D_MODEL), lambda i, e: (i // rows_per_mod, 0, 0)),
        ],
        out_specs=rowblk,
        out_shape=jax.ShapeDtypeStruct((n, D_MODEL), F32),
        scratch_shapes=[pltpu.VMEM((D_MODEL, tb), F32), pltpu.VMEM((_TE, tb), BF16)],
        compiler_params=_cparams(("parallel", "arbitrary")),
        name="peer_dense",
    )(h2, u_bf, vt_bf, s2, e2, th, cc, x1, gt2)


def _state_to_pairs(s):
    b = s.shape[0]
    return s.reshape(b, _N_PAIR, 2, HEAD_SIZE_B, HEAD_SIZE_B).transpose(0, 1, 3, 2, 4).reshape(b, _N_PAIR, HEAD_SIZE_B, LANES)


def _pairs_to_state(s):
    b = s.shape[0]
    return s.reshape(b, _N_PAIR, HEAD_SIZE_B, 2, HEAD_SIZE_B).transpose(0, 1, 3, 2, 4).reshape(b, N_HEADS_B, HEAD_SIZE_B, HEAD_SIZE_B)


def _stream(x, mods, w, prm, past, cfg):
    batch, t, _ = x.shape
    n = batch * t
    tm, rpm, tc, tb, rpm_tb = cfg
    sh1, sc1, gt1, sh2, sc2, gt2 = mods
    x2d = x.reshape(n, D_MODEL)
    q, k, v, pr, gates = _in_proj(x2d, sc1, sh1, prm['norm_mix_g'], w['w_in'], prm['q_norm_g'], prm['k_norm_g'], tm, rpm)
    if past is None:
        o_a = _attn_prompt(q, k, v, w['slopes'], prm['lambda_qk'], prm['subln_g'], batch, t, w['lam_init'])
        prev = jnp.zeros((batch, 1, C_SHIFT), F32)
        s0 = jnp.zeros((batch, _N_PAIR, HEAD_SIZE_B, LANES), F32)
    else:
        k_past, v_past, s_past, prev_row = past
        o_a = _attn_sample(q, k, v, k_past, v_past, w['slopes'],
                           prm['lambda_qk'], prm['subln_g'], batch, t, w['lam_init'])
        prev = prev_row.reshape(batch, 1, C_SHIFT)
        s0 = _state_to_pairs(s_past)
    o_b, s_pairs = _rwkv(pr, prev, s0, prm, batch, t, tc)
    x1, h2 = _merge(o_a, o_b, gates, x2d, gt1, sc2, sh2, prm['norm_ffn_g'], w['w_out'], tm if tm <= 512 else 512,
                    rpm if tm <= 512 else rpm * (tm // 512))
    s2, e2, th, cc = _peer_prep(h2, w['wq'], w['keys'], tb)
    y = _peer_dense(h2, w['u'], w['vt'], s2, e2, th, cc, x1, gt2, tb, rpm_tb)
    new_k = k.reshape(batch, t, N_HEADS_A, 2 * HEAD_DIM_A)
    new_v = v.reshape(batch, t, N_HEADS_A, 2 * HEAD_DIM_A)
    shift = pr.reshape(batch, t, C_SHIFT)[:, -1]
    return y.reshape(batch, t, D_MODEL), (new_k, new_v, _pairs_to_state(s_pairs), shift)


def kernel(x_prompt, x_sample, c_prompt, c_sample, cache_attn_k, cache_attn_v, state_rwkv_wkv, state_rwkv_shift, norm_mix_g, norm_ffn_g, w_ada, b_ada, w_in, q_norm_g, k_norm_g, lambda_qk, subln_g, shift_mu, decay_w0, decay_lora_up, iclr_a0, iclr_lora_up, gate_lora_up, k_k, k_a, r_k, lnx_g, lnx_b, w_out, peer_w_query, peer_sub_keys, peer_u, peer_v):
    depth = w_in.shape[0]
    bp, tp, _ = x_prompt.shape
    bs, ts, _ = x_sample.shape
    slopes = jnp.asarray([2.0 ** (-8.0 * (h + 1) / N_HEADS_A) for h in range(N_HEADS_A)], F32)
    st_p, st_s = [], []
    for l in range(depth):
        prm = {
            'norm_mix_g': norm_mix_g[l], 'norm_ffn_g': norm_ffn_g[l], 'q_norm_g': q_norm_g[l], 'k_norm_g': k_norm_g[l],
            'lambda_qk': lambda_qk[l], 'subln_g': subln_g[l], 'shift_mu': shift_mu[l], 'decay_w0': decay_w0[l],
            'decay_lora_up': decay_lora_up[l], 'iclr_a0': iclr_a0[l], 'iclr_lora_up': iclr_lora_up[l],
            'gate_lora_up': gate_lora_up[l], 'k_k': k_k[l], 'k_a': k_a[l], 'r_k': r_k[l], 'lnx_g': lnx_g[l], 'lnx_b': lnx_b[l],
        }
        w = {
            'w_in': w_in[l].astype(BF16),
            'w_out': w_out[l].astype(BF16),
            'wq': peer_w_query[l].reshape(D_MODEL, PEER_HEADS, 2 * N_KEYS).transpose(1, 0, 2).astype(BF16),
            'keys': peer_sub_keys[l].reshape(2 * PEER_HEADS, N_KEYS, N_KEYS).astype(BF16),
            'u': peer_u[l].astype(BF16),
            'vt': peer_v[l].T.astype(BF16),
            'slopes': slopes,
            'lam_init': 0.8 - 0.6 * math.exp(-0.3 * l),
        }
        ada = _ada(jnp.concatenate([c_prompt, c_sample], axis=0), w_ada[l], b_ada[l])
        mods = jnp.split(ada, 6, axis=-1)
        mods_p = [m[:bp].reshape(bp, 1, D_MODEL) for m in mods]
        mods_s = [jnp.repeat(m[bp:], ts, axis=0).reshape(1, bs * ts, D_MODEL) for m in mods]
        x_prompt, sp = _stream(x_prompt, mods_p, w, prm, None, (2048, tp // 2048, CHUNK, 512, tp // 512))
        x_sample, ss = _stream(x_sample, mods_s, w, prm,
                               (cache_attn_k[l], cache_attn_v[l], state_rwkv_wkv[l], state_rwkv_shift[l]),
                               (bs * ts, 1, ts, bs * ts, 1))
        st_p.append(sp)
        st_s.append(ss)
    stack = lambda sts, i: jnp.stack([s[i] for s in sts])
    return (x_prompt, x_sample, stack(st_p, 0), stack(st_p, 1), stack(st_p, 2), stack(st_p, 3),
            stack(st_s, 0), stack(st_s, 1), stack(st_s, 2), stack(st_s, 3))
```
